```python
import math
import numpy as np
import jax, jax.numpy as jnp
from jax import lax

D_MODEL = 1024
BATCH = 4
SEQ = 4096
DEPTH = 1
DEC_BATCH = 32
DEC_SEQ = 8
PAST_LEN = 8192
PAGE_SIZE = 128

D_MIX = D_MODEL
N_HEADS_ATT = 8
HEAD_DIM_ATT = (D_MIX // 2) // N_HEADS_ATT
ATT_W = N_HEADS_ATT * HEAD_DIM_ATT
N_IDX_HEADS = 4
IDX_DIM = 64
TOP_K_MAX = 256
Q_BLOCK = 128
N_HEADS_GDN = 4
HEAD_DIM_K = 128
HEAD_DIM_V = (D_MIX - ATT_W) // N_HEADS_GDN
GDN_K = N_HEADS_GDN * HEAD_DIM_K
GDN_V = N_HEADS_GDN * HEAD_DIM_V
CONV_W = 4
CONV_CH = 2 * GDN_K + GDN_V
CHUNK = 64
D_FF = 4 * D_MODEL
EPS = 1e-6
SPLIT_SIZES = (ATT_W, ATT_W, ATT_W, N_IDX_HEADS * IDX_DIM, IDX_DIM, N_IDX_HEADS,
               CONV_CH, N_HEADS_GDN, N_HEADS_GDN, GDN_V)
N_IN = 3 * ATT_W + N_IDX_HEADS * IDX_DIM + IDX_DIM + N_IDX_HEADS + CONV_CH + 2 * N_HEADS_GDN + GDN_V

kernel_name = 'hybrid_dsa_gated_deltanet_step'


def rmsnorm(x, g):
    x32 = x.astype(jnp.float32)
    y = x32 * lax.rsqrt(jnp.mean(x32 * x32, axis=-1, keepdims=True) + EPS)
    return (y * g.astype(jnp.float32)).astype(x.dtype)


def l2norm(x):
    return x * lax.rsqrt(jnp.sum(x * x, axis=-1, keepdims=True) + EPS)


def dsa_block(q, qi, wi, q_pos, k_all, v_all, ki_all, key_pos, n_sel):
    f32 = jnp.float32
    logits = jnp.einsum('bqhd,bsd->bqhs', qi, ki_all, preferred_element_type=f32) * (IDX_DIM ** -0.5)
    score = jnp.einsum('bqhs,bqh->bqs', jax.nn.relu(logits), wi.astype(f32))
    causal = key_pos[None, :] <= q_pos[:, None]
    score = jnp.where(causal[None], score, -jnp.inf)
    _, sel = lax.top_k(score, n_sel)
    valid = jnp.take(key_pos, sel) <= q_pos[None, :, None]
    gather = jax.vmap(lambda a, i: a[i])
    kg = gather(k_all, sel)
    vg = gather(v_all, sel)
    s = jnp.einsum('bqhd,bqkhd->bhqk', q, kg, preferred_element_type=f32) * (HEAD_DIM_ATT ** -0.5)
    s = jnp.where(valid[:, None], s, -jnp.inf)
    p = jax.nn.softmax(s, axis=-1)
    o = jnp.einsum('bhqk,bqkhd->bqhd', p.astype(vg.dtype), vg, preferred_element_type=f32)
    return o.astype(q.dtype)


def gated_delta_rule(q, k, v, g, beta, s0, chunk):
    B, T, H, DK = q.shape
    DV = v.shape[-1]
    n = T // chunk

    def chunks(a):
        a = a.reshape(B, n, chunk, H, *a.shape[3:])
        return jnp.moveaxis(a, 3, 1)

    q, k, v, g, beta = (chunks(a) for a in (q, k, v, g, beta))
    gc = jnp.cumsum(g, axis=-1)
    idx = jnp.arange(chunk)
    incl = idx[:, None] >= idx[None, :]
    strict = idx[:, None] > idx[None, :]
    decay = jnp.exp(jnp.where(incl, gc[..., :, None] - gc[..., None, :], -jnp.inf))
    kb = k * beta[..., None]
    vb = v * beta[..., None]
    lower = jnp.where(strict, jnp.einsum('bhncd,bhnsd->bhncs', kb, k) * decay, 0.0)
    a_mat = lower + jnp.eye(chunk, dtype=lower.dtype)
    rhs = jnp.concatenate([vb, kb * jnp.exp(gc)[..., None]], axis=-1)
    sol = lax.linalg.triangular_solve(a_mat, rhs, left_side=True, lower=True, unit_diagonal=True)
    u, w = sol[..., :DV], sol[..., DV:]
    aqk = jnp.where(incl, jnp.einsum('bhncd,bhnsd->bhncs', q, k) * decay, 0.0)

    def step(S, xs):
        qc, kc, uc, wc, gcc, ac = xs
        v_new = uc - jnp.einsum('bhck,bhkv->bhcv', wc, S)
        o = (jnp.einsum('bhck,bhkv->bhcv', qc * jnp.exp(gcc)[..., None], S)
             + jnp.einsum('bhcs,bhsv->bhcv', ac, v_new))
        g_last = gcc[..., -1]
        S = (S * jnp.exp(g_last)[..., None, None]
             + jnp.einsum('bhck,bhcv->bhkv', kc * jnp.exp(g_last[..., None] - gcc)[..., None], v_new))
        return S, o

    xs = tuple(jnp.moveaxis(a, 2, 0) for a in (q, k, u, w, gc, aqk))
    s_final, o = lax.scan(step, s0, xs)
    o = jnp.transpose(o, (1, 0, 3, 2, 4)).reshape(B, T, H, DV)
    return o, s_final


def hybrid_layer(x, k_past, v_past, ki_past, ssm0, conv0,
                 norm_mix, w_in, conv_w, a_log, dt_bias, norm_gdn_out, w_out, norm_mlp, w_up, w_down):
    B, T, _ = x.shape
    P = k_past.shape[1]
    f32 = jnp.float32
    h = rmsnorm(x, norm_mix)
    proj = h @ w_in
    offs = [int(o) for o in np.cumsum(SPLIT_SIZES)[:-1]]
    qa, ka, va, qi, ki, wi, qkv, a_in, b_in, z = jnp.split(proj, offs, axis=-1)

    qa = qa.reshape(B, T, N_HEADS_ATT, HEAD_DIM_ATT)
    ka = ka.reshape(B, T, N_HEADS_ATT, HEAD_DIM_ATT)
    va = va.reshape(B, T, N_HEADS_ATT, HEAD_DIM_ATT)
    qi = qi.reshape(B, T, N_IDX_HEADS, IDX_DIM)
    wi = wi * (N_IDX_HEADS ** -0.5)
    k_all = jnp.concatenate([k_past, ka], axis=1)
    v_all = jnp.concatenate([v_past, va], axis=1)
    ki_all = jnp.concatenate([ki_past, ki], axis=1)
    L = P + T
    n_sel = min(TOP_K_MAX, L // 4)
    key_pos = jnp.arange(L, dtype=jnp.int32)
    q_pos = P + jnp.arange(T, dtype=jnp.int32)
    attend = lambda xs: dsa_block(xs[0], xs[1], xs[2], xs[3], k_all, v_all, ki_all, key_pos, n_sel)
    qb = Q_BLOCK if T % Q_BLOCK == 0 else T
    nb = T // qb
    if nb == 1:
        o_a = attend((qa, qi, wi, q_pos))
    else:
        blk = lambda a: jnp.moveaxis(a.reshape(B, nb, qb, *a.shape[2:]), 1, 0)
        o_a = lax.map(attend, (blk(qa), blk(qi), blk(wi), q_pos.reshape(nb, qb)))
        o_a = jnp.moveaxis(o_a, 0, 1).reshape(B, T, N_HEADS_ATT, HEAD_DIM_ATT)

    xc = jnp.concatenate([conv0, qkv], axis=1)
    conv = xc[:, 0:T] * conv_w[0]
    for j in range(1, CONV_W):
        conv = conv + xc[:, j:j + T] * conv_w[j]
    conv = jax.nn.silu(conv)
    conv_new = xc[:, T:]
    qg, kg, vg = jnp.split(conv.astype(f32), [GDN_K, 2 * GDN_K], axis=-1)
    qg = l2norm(qg.reshape(B, T, N_HEADS_GDN, HEAD_DIM_K)) * (HEAD_DIM_K ** -0.5)
    kg = l2norm(kg.reshape(B, T, N_HEADS_GDN, HEAD_DIM_K))
    vg = vg.reshape(B, T, N_HEADS_GDN, HEAD_DIM_V)
    g = -jnp.exp(a_log.astype(f32)) * jax.nn.softplus(a_in.astype(f32) + dt_bias.astype(f32))
    beta = jax.nn.sigmoid(b_in.astype(f32))
    chunk = CHUNK if T % CHUNK == 0 else T
    o_b, ssm_new = gated_delta_rule(qg, kg, vg, g, beta, ssm0.astype(f32), chunk)
    zz = z.reshape(B, T, N_HEADS_GDN, HEAD_DIM_V).astype(f32)
    o_b = rmsnorm(o_b, norm_gdn_out) * jax.nn.silu(zz)

    mix = jnp.concatenate([o_a.reshape(B, T, ATT_W), o_b.reshape(B, T, GDN_V).astype(x.dtype)], axis=-1)
    x = x + mix @ w_out
    hm = rmsnorm(x, norm_mlp)
    x = x + jnp.square(jax.nn.relu(hm @ w_up)) @ w_down
    return x, ka, va, ki, ssm_new.astype(ssm0.dtype), conv_new


def setup_inputs(seed: int = 0) -> dict:
    key = jax.random.key(seed)
    ks = jax.random.split(key, 24)
    f32 = jnp.float32
    nrm = lambda k, shape, s: jax.random.normal(k, shape, f32) * s
    n_pages = PAST_LEN // PAGE_SIZE
    n_used = DEC_BATCH * n_pages
    n_phys = n_used + n_used // 4
    perm = jax.random.permutation(ks[0], n_phys).astype(jnp.int32)
    page_table = perm[:n_used].reshape(DEC_BATCH, n_pages)
    dt = jnp.exp(jax.random.uniform(ks[1], (DEPTH, N_HEADS_GDN), f32, math.log(1e-3), math.log(1e-1)))
    dt_bias = dt + jnp.log(-jnp.expm1(-dt))
    a_log = jnp.log(jax.random.uniform(ks[2], (DEPTH, N_HEADS_GDN), f32, 1.0, 16.0))
    return {
        'x_prompt': nrm(ks[3], (BATCH, SEQ, D_MODEL), 1.0),
        'x_sample': nrm(ks[4], (DEC_BATCH, DEC_SEQ, D_MODEL), 1.0),
        'cache_k': nrm(ks[5], (DEPTH, n_phys, PAGE_SIZE, N_HEADS_ATT, HEAD_DIM_ATT), 1.0),
        'cache_v': nrm(ks[6], (DEPTH, n_phys, PAGE_SIZE, N_HEADS_ATT, HEAD_DIM_ATT), 1.0),
        'cache_kidx': nrm(ks[7], (DEPTH, n_phys, PAGE_SIZE, IDX_DIM), 1.0),
        'state_ssm': nrm(ks[8], (DEPTH, DEC_BATCH, N_HEADS_GDN, HEAD_DIM_K, HEAD_DIM_V), HEAD_DIM_K ** -0.5),
        'state_conv': nrm(ks[9], (DEPTH, DEC_BATCH, CONV_W - 1, CONV_CH), 1.0),
        'page_table': page_table,
        'norm_mix': 1.0 + nrm(ks[10], (DEPTH, D_MODEL), 0.02),
        'w_in': nrm(ks[11], (DEPTH, D_MODEL, N_IN), D_MODEL ** -0.5),
        'conv_w': nrm(ks[12], (DEPTH, CONV_W, CONV_CH), CONV_W ** -0.5),
        'a_log': a_log,
        'dt_bias': dt_bias,
        'norm_gdn_out': 1.0 + nrm(ks[13], (DEPTH, HEAD_DIM_V), 0.02),
        'w_out': nrm(ks[14], (DEPTH, D_MIX, D_MODEL), D_MIX ** -0.5),
        'norm_mlp': 1.0 + nrm(ks[15], (DEPTH, D_MODEL), 0.02),
        'w_up': nrm(ks[16], (DEPTH, D_MODEL, D_FF), D_MODEL ** -0.5),
        'w_down': nrm(ks[17], (DEPTH, D_FF, D_MODEL), D_FF ** -0.5),
        'norm_final': 1.0 + nrm(ks[18], (D_MODEL,), 0.02),
    }


def reference(x_prompt, x_sample, cache_k, cache_v, cache_kidx, state_ssm, state_conv, page_table,
              norm_mix, w_in, conv_w, a_log, dt_bias, norm_gdn_out, w_out, norm_mlp, w_up, w_down, norm_final):
    bp = x_prompt.shape[0]
    bs = x_sample.shape[0]
    past = page_table.shape[1] * cache_k.shape[2]
    dt = x_prompt.dtype
    yp, ys = x_prompt, x_sample
    kp_l, vp_l, kip_l, sp_l, cp_l = [], [], [], [], []
    ks_l, vs_l, kis_l, ss_l, cs_l = [], [], [], [], []
    for l in range(DEPTH):
        w = (norm_mix[l], w_in[l], conv_w[l], a_log[l], dt_bias[l], norm_gdn_out[l],
             w_out[l], norm_mlp[l], w_up[l], w_down[l])
        yp, kp, vp, kip, sp, cp = hybrid_layer(
            yp,
            jnp.zeros((bp, 0, N_HEADS_ATT, HEAD_DIM_ATT), dt),
            jnp.zeros((bp, 0, N_HEADS_ATT, HEAD_DIM_ATT), dt),
            jnp.zeros((bp, 0, IDX_DIM), dt),
            jnp.zeros((bp, N_HEADS_GDN, HEAD_DIM_K, HEAD_DIM_V), state_ssm.dtype),
            jnp.zeros((bp, CONV_W - 1, CONV_CH), dt),
            *w)
        k_past = cache_k[l][page_table].reshape(bs, past, N_HEADS_ATT, HEAD_DIM_ATT)
        v_past = cache_v[l][page_table].reshape(bs, past, N_HEADS_ATT, HEAD_DIM_ATT)
        ki_past = cache_kidx[l][page_table].reshape(bs, past, IDX_DIM)
        ys, ks_, vs_, kis, ss, cs = hybrid_layer(ys, k_past, v_past, ki_past, state_ssm[l], state_conv[l], *w)
        kp_l.append(kp); vp_l.append(vp); kip_l.append(kip); sp_l.append(sp); cp_l.append(cp)
        ks_l.append(ks_); vs_l.append(vs_); kis_l.append(kis); ss_l.append(ss); cs_l.append(cs)
    y_prompt = rmsnorm(yp, norm_final)
    y_sample = rmsnorm(ys, norm_final)
    return (y_prompt, y_sample,
            jnp.stack(kp_l), jnp.stack(vp_l), jnp.stack(kip_l), jnp.stack(sp_l), jnp.stack(cp_l),
            jnp.stack(ks_l), jnp.stack(vs_l), jnp.stack(kis_l), jnp.stack(ss_l), jnp.stack(cs_l))
```

```python
import functools

import numpy as np
import jax
import jax.numpy as jnp
from jax import lax
from jax.experimental import pallas as pl
from jax.experimental.pallas import tpu as pltpu

F32 = jnp.float32
BF16 = jnp.bfloat16
I32 = jnp.int32
HI = lax.Precision.HIGHEST

EPS = 1e-6
N_HEADS_ATT = 8
HEAD_DIM_ATT = 64
ATT_W = N_HEADS_ATT * HEAD_DIM_ATT
N_IDX_HEADS = 4
IDX_DIM = 64
IDX_W = N_IDX_HEADS * IDX_DIM
TOP_K_MAX = 256
N_HEADS_GDN = 4
HEAD_DIM_K = 128
HEAD_DIM_V = 128
GDN_K = N_HEADS_GDN * HEAD_DIM_K
GDN_V = N_HEADS_GDN * HEAD_DIM_V
CONV_W = 4
CONV_CH = 2 * GDN_K + GDN_V
CHUNK = 64

LANES = 128
SUBLANES = 8
VMEM_LIMIT = 56 * 1024 * 1024

PJ_ATT = 0
PJ_CV = PJ_ATT + 3 * ATT_W
PJ_Z = PJ_CV + CONV_CH
PJ_QI = PJ_Z + GDN_V
PJ_KI = PJ_QI + IDX_W
PJ_GATE = PJ_KI + 2 * IDX_DIM
PJ_W = PJ_GATE + LANES
GATE_WI = 0
GATE_A = 4
GATE_B = 8

INT_MIN = np.int32(-2 ** 31)
KEY_NEG = np.int32(np.uint32(0x807FFFFF).view(np.int32))
NEG_INF = float("-inf")

NT_DIMS = (((1,), (1,)), ((), ()))
TN_DIMS = (((0,), (0,)), ((), ()))


def _key_to_float(key):
    bits = jnp.where(key < 0, key ^ np.int32(0x7FFFFFFF), key)
    return jnp.where(key < KEY_NEG, NEG_INF, lax.bitcast_convert_type(bits, F32))


def _sigmoid(x):
    return 1.0 / (1.0 + jnp.exp(-x))


def _softplus(x):
    return jnp.maximum(x, 0.0) + jnp.log1p(jnp.exp(-jnp.abs(x)))


def _rms(x, g):
    return x * lax.rsqrt(jnp.mean(x * x, axis=-1, keepdims=True) + EPS) * g


def _const_spec(shape):
    return pl.BlockSpec(shape, lambda *_: (0,) * len(shape), pipeline_mode=pl.Buffered(1))


def _params(n_grid_dims):
    return pltpu.CompilerParams(dimension_semantics=("arbitrary",) * n_grid_dims, vmem_limit_bytes=VMEM_LIMIT)


def _inproj_kernel(cache_t, x_ref, g_ref, w_ref, *out_refs):
    hb = _rms(x_ref[...], g_ref[...]).astype(BF16)
    proj = jnp.dot(hb, w_ref[...], preferred_element_type=F32)
    qa = proj[:, PJ_ATT:PJ_ATT + ATT_W]
    ka = proj[:, PJ_ATT + ATT_W:PJ_ATT + 2 * ATT_W]
    va = proj[:, PJ_ATT + 2 * ATT_W:PJ_ATT + 3 * ATT_W]
    qkv = proj[:, PJ_CV:PJ_CV + CONV_CH]
    z = proj[:, PJ_Z:PJ_Z + GDN_V]
    qi = proj[:, PJ_QI:PJ_QI + IDX_W]
    ki2 = proj[:, PJ_KI:PJ_KI + 2 * IDX_DIM]
    gates = proj[:, PJ_GATE:PJ_GATE + LANES]
    if cache_t:
        (qab_ref, kab_ref, vtb_ref, kat_ref, vat_ref, kit_ref, qkv_ref, z_ref, qib_ref, ki2b_ref, gate_ref) = out_refs
        tm = ka.shape[0]
        kab_ref[...] = ka.astype(BF16)
        ki2b_ref[...] = ki2.astype(BF16)
        vat = va.T
        vtb_ref[...] = vat.astype(BF16)
        kat_ref[...] = ka.T.reshape(N_HEADS_ATT, HEAD_DIM_ATT, tm)
        vat_ref[...] = vat.reshape(N_HEADS_ATT, HEAD_DIM_ATT, tm)
        kit_ref[...] = ki2.T[:IDX_DIM, :]
    else:
        (qab_ref, ka_ref, va_ref, ki_ref, qkv_ref, z_ref, qib_ref, gate_ref) = out_refs
        ka_ref[...] = ka
        va_ref[...] = va
        ki_ref[...] = ki2[:, :IDX_DIM]
    qab_ref[...] = qa.astype(BF16)
    qkv_ref[...] = qkv
    z_ref[...] = z
    qib_ref[...] = qi.astype(BF16)
    gate_ref[...] = gates


def _inproj(x2d, g, w, tm, seq_len, cache_t):
    n, d = x2d.shape
    row = lambda w_: pl.BlockSpec((tm, w_), lambda i: (i, 0))
    sds = jax.ShapeDtypeStruct
    common = [(row(CONV_CH), sds((n, CONV_CH), F32)), (row(GDN_V), sds((n, GDN_V), F32)),
              (row(IDX_W), sds((n, IDX_W), BF16)), (row(LANES), sds((n, LANES), F32))]
    if cache_t:
        assert seq_len % tm == 0
        n_t = seq_len // tm
        b = n // seq_len
        tok = lambda *lead: pl.BlockSpec((None,) + lead + (tm,), lambda i: (i // n_t,) + (0,) * len(lead) + (i % n_t,))
        head_t = (N_HEADS_ATT, HEAD_DIM_ATT)
        out = [(row(ATT_W), sds((n, ATT_W), BF16)),
               (row(ATT_W), sds((n, ATT_W), BF16)),
               (pl.BlockSpec((None, None, ATT_W, tm), lambda i: (i // n_t, i % n_t, 0, 0)),
                sds((b, n_t, ATT_W, tm), BF16)),
               (tok(*head_t), sds((b,) + head_t + (seq_len,), F32)),
               (tok(*head_t), sds((b,) + head_t + (seq_len,), F32)),
               (tok(IDX_DIM), sds((b, IDX_DIM, seq_len), F32)),
               common[0], common[1], common[2],
               (row(2 * IDX_DIM), sds((n, 2 * IDX_DIM), BF16)),
               common[3]]
    else:
        out = [(row(ATT_W), sds((n, ATT_W), BF16)), (row(ATT_W), sds((n, ATT_W), F32)),
               (row(ATT_W), sds((n, ATT_W), F32)), (row(IDX_DIM), sds((n, IDX_DIM), F32))] + common
    return pl.pallas_call(
        functools.partial(_inproj_kernel, cache_t),
        grid=(n // tm,),
        in_specs=[row(d), _const_spec((1, d)), _const_spec(w.shape)],
        out_specs=tuple(o[0] for o in out),
        out_shape=tuple(o[1] for o in out),
        compiler_params=_params(1),
        name="inproj",
    )(x2d, g, w)


def _mlp_kernel(ff_slab, final_norm, x_ref, oa_ref, ob_ref, wout_ref, gm_ref, wup_ref, wdn_ref, gf_ref, y_ref):
    mix = jnp.concatenate([oa_ref[...], ob_ref[...]], axis=1)
    x1 = x_ref[...] + jnp.dot(mix, wout_ref[...], preferred_element_type=F32)
    hm = _rms(x1, gm_ref[...]).astype(BF16)
    acc = x1
    d_ff = wup_ref.shape[1]
    for s in range(0, d_ff, ff_slab):
        up = jnp.dot(hm, wup_ref[:, s:s + ff_slab], preferred_element_type=F32)
        act = jnp.square(jnp.maximum(up, 0.0)).astype(BF16)
        acc = acc + jnp.dot(act, wdn_ref[s:s + ff_slab, :], preferred_element_type=F32)
    y_ref[...] = _rms(acc, gf_ref[...]) if final_norm else acc


def _mlp_layer(x2d, oa, ob, w_out, g_mlp, w_up, w_dn, g_final, final_norm, tm):
    n, d = x2d.shape
    row = lambda w: pl.BlockSpec((tm, w), lambda i: (i, 0))
    return pl.pallas_call(
        functools.partial(_mlp_kernel, 1024, final_norm),
        grid=(n // tm,),
        in_specs=[row(d), row(ATT_W), row(GDN_V), _const_spec(w_out.shape), _const_spec((1, d)),
                  _const_spec(w_up.shape), _const_spec(w_dn.shape), _const_spec((1, d))],
        out_specs=row(d),
        out_shape=jax.ShapeDtypeStruct((n, d), F32),
        compiler_params=_params(1),
        name="outproj_mlp",
    )(x2d, oa, ob, w_out, g_mlp, w_up, w_dn, g_final)


def _gdn_kernel(t_valid, bb, qkv_ref, conv0_ref, gate_ref, z_ref, s0_ref, cw_ref, alog_ref, dtb_ref, gn_ref,
                ob_ref, sout_ref, ext_ref, s_ref):
    c = pl.program_id(1)
    n_c = pl.num_programs(1)
    C = CHUNK
    HIST = SUBLANES

    @pl.when(c == 0)
    def _():
        for b in range(bb):
            ext_ref[b, 0:HIST, :] = jnp.zeros((HIST, CONV_CH), F32)
            ext_ref[b, HIST - (CONV_W - 1):HIST, :] = conv0_ref[b]
        s_ref[...] = s0_ref[...]

    rows = lax.broadcasted_iota(I32, (C, C), 0)
    cols = lax.broadcasted_iota(I32, (C, C), 1)
    incl = rows >= cols
    strict = rows > cols
    l_incl = incl.astype(F32)
    eye = (rows == cols).astype(F32)
    gn = gn_ref[...]
    n_double = int(np.log2(C)) - 1

    for b in range(bb):
        ext_ref[b, HIST:HIST + C, :] = qkv_ref[b]
        conv = ext_ref[b, HIST - 3:HIST - 3 + C, :] * cw_ref[0:1, :]
        for j in range(1, CONV_W):
            conv = conv + ext_ref[b, HIST - 3 + j:HIST - 3 + j + C, :] * cw_ref[j:j + 1, :]
        conv = conv * _sigmoid(conv)
        ext_ref[b, 0:HIST, :] = ext_ref[b, C:C + HIST, :]

        gates = gate_ref[b]
        g_slab = -jnp.exp(alog_ref[...]) * _softplus(gates + dtb_ref[...])
        beta_slab = _sigmoid(gates)
        if t_valid < C:
            live = lax.broadcasted_iota(I32, (C, LANES), 0) < t_valid
            g_slab = jnp.where(live, g_slab, 0.0)
            beta_slab = jnp.where(live, beta_slab, 0.0)
        gc_slab = jnp.dot(l_incl, g_slab, precision=HI, preferred_element_type=F32)
        gc_rows = gc_slab.T
        z = z_ref[b]

        for h in range(N_HEADS_GDN):
            q = conv[:, h * HEAD_DIM_K:(h + 1) * HEAD_DIM_K]
            k = conv[:, GDN_K + h * HEAD_DIM_K:GDN_K + (h + 1) * HEAD_DIM_K]
            v = conv[:, 2 * GDN_K + h * HEAD_DIM_V:2 * GDN_K + (h + 1) * HEAD_DIM_V]
            q = q * lax.rsqrt(jnp.sum(q * q, axis=-1, keepdims=True) + EPS) * (HEAD_DIM_K ** -0.5)
            k = k * lax.rsqrt(jnp.sum(k * k, axis=-1, keepdims=True) + EPS)
            gc_h = gc_slab[:, GATE_A + h:GATE_A + h + 1]
            beta = beta_slab[:, GATE_B + h:GATE_B + h + 1]
            eg = jnp.exp(gc_h)
            kb = k * beta
            vb = v * beta
            diff = gc_h - gc_rows[GATE_A + h:GATE_A + h + 1, :]
            decay = jnp.exp(jnp.where(incl, diff, NEG_INF))
            kk = lax.dot_general(kb, k, NT_DIMS, preferred_element_type=F32)
            qk = lax.dot_general(q, k, NT_DIMS, preferred_element_type=F32)
            lower = jnp.where(strict, kk * decay, 0.0)
            aqk = jnp.where(incl, qk * decay, 0.0)
            pw = -lower
            tinv = eye + pw
            for _ in range(n_double):
                pw = jnp.dot(pw, pw, precision=HI, preferred_element_type=F32)
                tinv = tinv + jnp.dot(tinv, pw, precision=HI, preferred_element_type=F32)
            u = jnp.dot(tinv, vb, precision=HI, preferred_element_type=F32)
            w = jnp.dot(tinv, kb * eg, precision=HI, preferred_element_type=F32)
            s_h = s_ref[b, h]
            v_new = u - jnp.dot(w, s_h, preferred_element_type=F32)
            o = (jnp.dot(q * eg, s_h, preferred_element_type=F32)
                 + jnp.dot(aqk, v_new, preferred_element_type=F32))
            g_last = gc_h[C - 1:C, :]
            kd = k * jnp.exp(g_last - gc_h)
            s_ref[b, h] = (s_h * jnp.exp(g_last)
                           + lax.dot_general(kd, v_new, TN_DIMS, preferred_element_type=F32))
            zz = z[:, h * HEAD_DIM_V:(h + 1) * HEAD_DIM_V]
            o = _rms(o, gn) * (zz * _sigmoid(zz))
            ob_ref[b, :, h * HEAD_DIM_V:(h + 1) * HEAD_DIM_V] = o.astype(ob_ref.dtype)

    @pl.when(c == n_c - 1)
    def _():
        sout_ref[...] = s_ref[...]


def _gdn(qkv, conv0, gates, z, s0, conv_w, alog_row, dtb_row, gn, t_valid, bb):
    b, t, _ = qkv.shape
    n_c = t // CHUNK
    blk = lambda w: pl.BlockSpec((bb, CHUNK, w), lambda i, c: (i, c, 0))
    per_b = lambda shape: pl.BlockSpec((bb,) + shape, lambda i, c: (i,) + (0,) * len(shape))
    state = (N_HEADS_GDN, HEAD_DIM_K, HEAD_DIM_V)
    return pl.pallas_call(
        functools.partial(_gdn_kernel, t_valid, bb),
        grid=(b // bb, n_c),
        in_specs=[blk(CONV_CH), per_b((CONV_W - 1, CONV_CH)), blk(LANES), blk(GDN_V), per_b(state),
                  _const_spec((CONV_W, CONV_CH)), _const_spec((1, LANES)), _const_spec((1, LANES)),
                  _const_spec((1, HEAD_DIM_V))],
        out_specs=(blk(GDN_V), per_b(state)),
        out_shape=(jax.ShapeDtypeStruct((b, t, GDN_V), BF16), jax.ShapeDtypeStruct((b,) + state, F32)),
        scratch_shapes=[pltpu.VMEM((bb, CHUNK + 2 * SUBLANES, CONV_CH), F32), pltpu.VMEM((bb,) + state, F32)],
        compiler_params=_params(2),
        name="gdn",
    )(qkv, conv0, gates, z, s0, conv_w, alog_row, dtb_row, gn)


def _topk_rule(count_ge, count_tie_le, qshape, n_sel, n_idx_bits, j_scr):
    def bis(it, u):
        cu = u | lax.shift_left(jnp.int32(1), 31 - it)
        return jnp.where(count_ge(_key_to_float(cu ^ INT_MIN)) >= n_sel, cu, u)

    thr_key = lax.fori_loop(0, 32, bis, jnp.zeros(qshape, I32)) ^ INT_MIN
    thr = _key_to_float(thr_key)
    nxt = _key_to_float(thr_key + 1)
    need = n_sel - count_ge(nxt)
    excess = jnp.logical_and(count_ge(thr) > n_sel, thr_key > KEY_NEG)
    j_scr[...] = jnp.full(qshape, np.int32(2 ** 30), I32)

    @pl.when(jnp.max(excess.astype(I32)) > 0)
    def _():
        def jbis(it, j):
            cj = j | lax.shift_left(jnp.int32(1), n_idx_bits - 1 - it)
            return jnp.where(count_tie_le(thr, nxt, cj - 1) < need, cj, j)

        j_scr[...] = lax.fori_loop(0, n_idx_bits, jbis, jnp.zeros(qshape, I32))

    return thr, nxt, j_scr[...]


def _dsa_prompt_kernel(n_sel, tq, ck, t_total, qi_ref, ki2_ref, gate_ref, q_ref, k_ref, vt_ref,
                       o_ref, sc_scr, s_scr, j_scr):
    i = pl.program_id(1)
    n_chunks = lax.div(i * tq + tq + ck - 1, ck)
    grp = ck // SUBLANES
    g3 = (grp, SUBLANES, tq)
    q_id = i * tq + lax.broadcasted_iota(I32, g3, 2)
    k_in_chunk = lax.broadcasted_iota(I32, g3, 0) * SUBLANES + lax.broadcasted_iota(I32, g3, 1)
    lane_lo = lax.broadcasted_iota(I32, (tq, LANES), 1) < HEAD_DIM_ATT
    rep8 = lambda v: jnp.broadcast_to(v, (SUBLANES, tq))

    qi = qi_ref[...] * (IDX_DIM ** -0.5)
    wi_t = gate_ref[...].T[GATE_WI:GATE_WI + N_IDX_HEADS, :] * (N_IDX_HEADS ** -0.5)
    qi_heads = []
    for h in range(N_IDX_HEADS):
        pair = qi[:, (h // 2) * LANES:(h // 2 + 1) * LANES]
        keep = lane_lo if h % 2 == 0 else jnp.logical_not(lane_lo)
        qi_heads.append(jnp.where(keep, pair, jnp.zeros_like(pair)))

    def score_body(c, carry):
        k0 = pl.multiple_of(c * ck, ck)
        kic = ki2_ref[pl.ds(k0, ck), :]
        sc = jnp.zeros((ck, tq), F32)
        for h in range(N_IDX_HEADS):
            logit = lax.dot_general(kic, qi_heads[h], NT_DIMS, preferred_element_type=F32)
            sc = sc + jnp.maximum(logit, 0.0) * wi_t[h:h + 1, :]
        causal = k0 + k_in_chunk <= q_id
        sc_scr[c] = jnp.where(causal, sc.reshape(g3), NEG_INF).reshape(ck, tq)
        return carry

    lax.fori_loop(0, n_chunks, score_body, 0)

    def count_ge(cand):
        cand8 = rep8(cand)
        body = lambda c, acc: acc + jnp.sum(jnp.where(sc_scr[c].reshape(g3) >= cand8, 1.0, 0.0), axis=0)
        acc = lax.fori_loop(0, n_chunks, body, jnp.zeros((SUBLANES, tq), F32))
        return jnp.sum(acc, axis=0, keepdims=True)

    def count_tie_le(thr, nxt, j):
        thr8, nxt8, j8 = rep8(thr), rep8(nxt), rep8(j)

        def body(c, acc):
            s = sc_scr[c].reshape(g3)
            hit = jnp.logical_and(jnp.logical_and(s >= thr8, jnp.logical_not(s >= nxt8)), c * ck + k_in_chunk <= j8)
            return acc + jnp.sum(jnp.where(hit, 1.0, 0.0), axis=0)

        acc = lax.fori_loop(0, n_chunks, body, jnp.zeros((SUBLANES, tq), F32))
        return jnp.sum(acc, axis=0, keepdims=True)

    thr, nxt, jmax = _topk_rule(count_ge, count_tie_le, (1, tq), n_sel, int(np.ceil(np.log2(t_total))), j_scr)
    thr8, nxt8, j8 = rep8(thr), rep8(nxt), rep8(jmax)

    def mask_body(c, carry):
        s = sc_scr[c].reshape(g3)
        sel = jnp.logical_or(s >= nxt8, jnp.logical_and(s >= thr8, c * ck + k_in_chunk <= j8))
        sel = jnp.logical_and(sel, s > NEG_INF)
        sc_scr[c] = jnp.where(sel, 0.0, NEG_INF).reshape(ck, tq)
        return carry

    lax.fori_loop(0, n_chunks, mask_body, 0)

    o_rows = []
    for h in range(N_HEADS_ATT):
        p = h // 2
        qp = q_ref[:, p * LANES:(p + 1) * LANES] * (HEAD_DIM_ATT ** -0.5)
        keep = lane_lo if h % 2 == 0 else jnp.logical_not(lane_lo)
        qm = jnp.where(keep, qp, jnp.zeros_like(qp))

        def s_body(c, macc):
            k0 = pl.multiple_of(c * ck, ck)
            kc = k_ref[pl.ds(k0, ck), p * LANES:(p + 1) * LANES]
            st = lax.dot_general(kc, qm, NT_DIMS, preferred_element_type=F32) + sc_scr[c]
            s_scr[c] = st
            return jnp.maximum(macc, jnp.max(st.reshape(g3), axis=0))

        macc = lax.fori_loop(0, n_chunks, s_body, jnp.full((SUBLANES, tq), NEG_INF, F32))
        m = jnp.max(macc, axis=0, keepdims=True)

        def pv_body(c, carry):
            lacc, oacc = carry
            pt = jnp.exp(s_scr[c] - m)
            lacc = lacc + jnp.sum(pt.reshape(g3), axis=0)
            vt = vt_ref[c, h * HEAD_DIM_ATT:(h + 1) * HEAD_DIM_ATT, :]
            return lacc, oacc + jnp.dot(vt, pt.astype(BF16), preferred_element_type=F32)

        lacc, oacc = lax.fori_loop(0, n_chunks, pv_body,
                                   (jnp.zeros((SUBLANES, tq), F32), jnp.zeros((HEAD_DIM_ATT, tq), F32)))
        o_rows.append(oacc / jnp.sum(lacc, axis=0, keepdims=True))
    o_ref[...] = jnp.concatenate(o_rows, axis=0).T.astype(o_ref.dtype)


def _dsa_prompt(qib, ki2b, gates, qab, kab, vtb, batch, t, n_sel, tq, ck):
    n_q = t // tq
    tile = lambda w: pl.BlockSpec((tq, w), lambda b, i: (b * n_q + i, 0))
    seq = lambda w: pl.BlockSpec((t, w), lambda b, i: (b, 0))
    return pl.pallas_call(
        functools.partial(_dsa_prompt_kernel, n_sel, tq, ck, t),
        grid=(batch, n_q),
        in_specs=[tile(IDX_W), seq(2 * IDX_DIM), tile(LANES), tile(ATT_W), seq(ATT_W),
                  pl.BlockSpec((None, t // ck, ATT_W, ck), lambda b, i: (b, 0, 0, 0))],
        out_specs=tile(ATT_W),
        out_shape=jax.ShapeDtypeStruct((batch * t, ATT_W), BF16),
        scratch_shapes=[pltpu.VMEM((t // ck, ck, tq), F32), pltpu.VMEM((t // ck, ck, tq), F32),
                        pltpu.VMEM((1, tq), I32)],
        compiler_params=_params(2),
        name="dsa_prompt",
    )(qib, ki2b, gates, qab, kab, vtb)


def _dsa_sample_kernel(n_sel, n_pages, pg, pt_ref, qi_ref, wcol_ref, kinew_ref, q_ref, knew_ref, vnew_ref, *refs):
    ki_refs, k_refs, v_refs = refs[:pg], refs[pg:2 * pg], refs[2 * pg:3 * pg]
    o_ref = refs[3 * pg]
    sc_scr, s_scr, macc_scr, lacc_scr, oacc_scr, j_scr = refs[3 * pg + 1:]
    del pt_ref
    s = pl.program_id(1)
    nst = n_pages // pg
    n_ch = n_pages + 1
    tq = SUBLANES
    page = LANES
    qshape = (tq, page)
    lane = lax.broadcasted_iota(I32, qshape, 1)

    def new_t(x):
        wdt = x.shape[1]
        wp = max(wdt, page)
        if wp > wdt:
            x = jnp.concatenate([x, jnp.zeros((tq, wp - wdt), F32)], axis=1)
        sq = jnp.concatenate([x, jnp.zeros((page - tq, wp), F32)], axis=0)
        return jnp.concatenate([sq[:, t * page:(t + 1) * page].T for t in range(wp // page)], axis=0)[:wdt]

    row_total = lambda acc: jnp.dot(acc.astype(BF16), jnp.ones((page, page), BF16), preferred_element_type=F32)

    @pl.when(s < nst)
    def _():
        qi = qi_ref[...] * (IDX_DIM ** -0.5)
        wcol = wcol_ref[...] * (N_IDX_HEADS ** -0.5)

        def score_chunk(ki_t):
            logit = jnp.dot(qi, ki_t.astype(BF16), preferred_element_type=F32)
            lw = jnp.maximum(logit, 0.0) * wcol
            sc = lw[0:tq]
            for h in range(1, N_IDX_HEADS):
                sc = sc + lw[h * tq:(h + 1) * tq]
            return sc

        for p in range(pg):
            sc_scr[s * pg + p] = score_chunk(ki_refs[p][...])

        @pl.when(s == 0)
        def _():
            sc = score_chunk(new_t(kinew_ref[...]))
            causal = lane <= lax.broadcasted_iota(I32, qshape, 0)
            sc_scr[n_pages] = jnp.where(causal, sc, NEG_INF)

        @pl.when(s == nst - 1)
        def _():
            def count_ge(cand):
                acc = lax.fori_loop(0, n_ch, lambda c, a: a + jnp.where(sc_scr[c] >= cand, 1.0, 0.0),
                                    jnp.zeros(qshape, F32))
                return row_total(acc)

            def tie(sc, thr, nxt, c, j):
                return jnp.logical_and(jnp.logical_and(sc >= thr, jnp.logical_not(sc >= nxt)), c * page + lane <= j)

            def count_tie_le(thr, nxt, j):
                body = lambda c, a: a + jnp.where(tie(sc_scr[c], thr, nxt, c, j), 1.0, 0.0)
                return row_total(lax.fori_loop(0, n_ch, body, jnp.zeros(qshape, F32)))

            thr, nxt, jmax = _topk_rule(count_ge, count_tie_le, qshape, n_sel,
                                        int(np.ceil(np.log2(n_ch * page))), j_scr)

            def bias_body(c, carry):
                sc = sc_scr[c]
                sel = jnp.logical_or(sc >= nxt, tie(sc, thr, nxt, c, jmax))
                sel = jnp.logical_and(sel, sc > NEG_INF)
                sc_scr[c] = jnp.where(sel, 0.0, NEG_INF)
                return carry

            lax.fori_loop(0, n_ch, bias_body, 0)

    @pl.when(jnp.logical_and(s >= nst, s < 2 * nst))
    def _():
        j = s - nst
        q = q_ref[...] * (HEAD_DIM_ATT ** -0.5)

        def s_chunk(kt_of_head, c):
            bias = sc_scr[c]
            parts = []
            for h in range(N_HEADS_ATT):
                qh = q[:, h * HEAD_DIM_ATT:(h + 1) * HEAD_DIM_ATT]
                parts.append(jnp.dot(qh, kt_of_head(h).astype(BF16), preferred_element_type=F32) + bias)
            sc = jnp.concatenate(parts, axis=0)
            s_scr[c] = sc
            return sc

        @pl.when(j == 0)
        def _():
            knew_t = new_t(knew_ref[...])
            macc_scr[...] = s_chunk(lambda h: knew_t[h * HEAD_DIM_ATT:(h + 1) * HEAD_DIM_ATT], n_pages)

        macc = macc_scr[...]
        for p in range(pg):
            macc = jnp.maximum(macc, s_chunk(lambda h, p=p: k_refs[p][h], j * pg + p))
        macc_scr[...] = macc

    @pl.when(s >= 2 * nst)
    def _():
        j = s - 2 * nst
        m = jnp.max(macc_scr[...], axis=1, keepdims=True)

        def pv_chunk(vt_of_head, c):
            pch = jnp.exp(s_scr[c] - m)
            pb = pch.astype(BF16)
            outs = [lax.dot_general(pb[h * tq:(h + 1) * tq], vt_of_head(h).astype(BF16), NT_DIMS,
                                    preferred_element_type=F32) for h in range(N_HEADS_ATT)]
            return pch, jnp.concatenate(outs, axis=0)

        @pl.when(j == 0)
        def _():
            vnew_t = new_t(vnew_ref[...])
            l0, o0 = pv_chunk(lambda h: vnew_t[h * HEAD_DIM_ATT:(h + 1) * HEAD_DIM_ATT], n_pages)
            lacc_scr[...] = l0
            oacc_scr[...] = o0

        lacc = lacc_scr[...]
        oacc = oacc_scr[...]
        for p in range(pg):
            lp, op = pv_chunk(lambda h, p=p: v_refs[p][h], j * pg + p)
            lacc = lacc + lp
            oacc = oacc + op
        lacc_scr[...] = lacc
        oacc_scr[...] = oacc

        @pl.when(j == nst - 1)
        def _():
            o = oacc / jnp.sum(lacc, axis=1, keepdims=True)
            o_ref[...] = jnp.concatenate([o[h * tq:(h + 1) * tq] for h in range(N_HEADS_ATT)], axis=1)


def _dsa_sample(layer, page_table, qi, wcol, ki_new, q, k_new, v_new, cache_ki_t, cache_k_t, cache_v_t, n_sel, pg):
    b, n_pages = page_table.shape
    page = cache_k_t.shape[-1]
    assert page == LANES and n_pages % pg == 0 and ki_new.shape[1] == SUBLANES
    nst = n_pages // pg
    per_b = lambda shape: pl.BlockSpec((None,) + shape, lambda i, s, pt: (i,) + (0,) * len(shape))

    def page_spec(lead, phase, p):
        def idx(i, s, pt):
            step = jnp.clip(s - phase * nst, 0, nst - 1)
            return (layer, pt[i, step * pg + p]) + (0,) * (len(lead) + 1)
        return pl.BlockSpec((None, None) + lead + (page,), idx)

    hq_idx = N_IDX_HEADS * SUBLANES
    in_specs = [per_b((hq_idx, IDX_DIM)), per_b((hq_idx, 1)), per_b((SUBLANES, IDX_DIM)), per_b((SUBLANES, ATT_W)),
                per_b((SUBLANES, ATT_W)), per_b((SUBLANES, ATT_W))]
    in_specs += [page_spec((IDX_DIM,), 0, p) for p in range(pg)]
    in_specs += [page_spec((N_HEADS_ATT, HEAD_DIM_ATT), 1, p) for p in range(pg)]
    in_specs += [page_spec((N_HEADS_ATT, HEAD_DIM_ATT), 2, p) for p in range(pg)]
    hq = N_HEADS_ATT * SUBLANES
    grid_spec = pltpu.PrefetchScalarGridSpec(
        num_scalar_prefetch=1,
        grid=(b, 3 * nst),
        in_specs=in_specs,
        out_specs=per_b((SUBLANES, ATT_W)),
        scratch_shapes=[pltpu.VMEM((n_pages + 1, SUBLANES, LANES), F32), pltpu.VMEM((n_pages + 1, hq, LANES), F32),
                        pltpu.VMEM((hq, LANES), F32), pltpu.VMEM((hq, LANES), F32),
                        pltpu.VMEM((hq, HEAD_DIM_ATT), F32), pltpu.VMEM((SUBLANES, LANES), I32)],
    )
    return pl.pallas_call(
        functools.partial(_dsa_sample_kernel, n_sel, n_pages, pg),
        grid_spec=grid_spec,
        out_shape=jax.ShapeDtypeStruct((b, SUBLANES, ATT_W), F32),
        compiler_params=_params(2),
        name="dsa_sample",
    )(page_table, qi, wcol, ki_new, q, k_new, v_new,
      *([cache_ki_t] * pg), *([cache_k_t] * pg), *([cache_v_t] * pg))


def _regroup_w_in(w):
    o_qi = 3 * ATT_W
    o_ki = o_qi + IDX_W
    o_wi = o_ki + IDX_DIM
    o_cv = o_wi + N_IDX_HEADS
    o_a = o_cv + CONV_CH
    o_b = o_a + N_HEADS_GDN
    o_z = o_b + N_HEADS_GDN
    gate_cols = jnp.concatenate([w[:, o_wi:o_cv], w[:, o_a:o_b], w[:, o_b:o_z]], axis=1)
    gate_cols = jnp.pad(gate_cols, ((0, 0), (0, LANES - gate_cols.shape[1])))
    out = jnp.concatenate([w[:, :o_qi], w[:, o_cv:o_a], w[:, o_z:o_z + GDN_V], w[:, o_qi:o_ki],
                           w[:, o_ki:o_wi], w[:, o_ki:o_wi], gate_cols], axis=1)
    assert out.shape[1] == PJ_W
    return out.astype(BF16)


def _gate_row(v):
    return jnp.pad(v.astype(F32), (GATE_A, LANES - GATE_A - v.shape[0]))[None, :]


def _tiles(bp, tp, bs, ts, n_pages):
    ck = min(512, tp)
    return dict(
        tm_p=ck,
        tm_s=min(256, bs * ts),
        tq=min(256, tp),
        ck=ck,
        bb_p=min(4, bp),
        bb_s=min(4, bs),
        pg=min(16, n_pages),
    )


def kernel(x_prompt, x_sample, cache_k, cache_v, cache_kidx, state_ssm, state_conv, page_table, norm_mix, w_in,
           conv_w, a_log, dt_bias, norm_gdn_out, w_out, norm_mlp, w_up, w_down, norm_final):
    depth = w_in.shape[0]
    bp, tp, d = x_prompt.shape
    bs, ts, _ = x_sample.shape
    page = cache_k.shape[2]
    past = page_table.shape[1] * page
    cfg = _tiles(bp, tp, bs, ts, page_table.shape[1])
    n_sel_p = min(TOP_K_MAX, tp // 4)
    n_sel_s = min(TOP_K_MAX, (past + ts) // 4)
    assert tp % CHUNK == 0 and CONV_W - 1 <= ts <= CHUNK
    assert cfg["tq"] >= n_sel_p and tp % cfg["tq"] == 0 and tp % cfg["ck"] == 0 and cfg["ck"] % cfg["tq"] == 0
    assert bp % cfg["bb_p"] == 0 and bs % cfg["bb_s"] == 0

    cache_k_t = jnp.transpose(cache_k, (0, 1, 3, 4, 2))
    cache_v_t = jnp.transpose(cache_v, (0, 1, 3, 4, 2))
    cache_ki_t = jnp.transpose(cache_kidx, (0, 1, 3, 2))

    yp = x_prompt.reshape(bp * tp, d)
    ys = x_sample.reshape(bs * ts, d)
    outs = {k: [] for k in ("kp", "vp", "kip", "sp", "cp", "ks", "vs", "kis", "ss", "cs")}
    for l in range(depth):
        w_pj = _regroup_w_in(w_in[l])
        g_mix = norm_mix[l][None, :]
        alog_row, dtb_row = _gate_row(a_log[l]), _gate_row(dt_bias[l])
        gn = norm_gdn_out[l][None, :]
        w_out_b, w_up_b, w_dn_b = w_out[l].astype(BF16), w_up[l].astype(BF16), w_down[l].astype(BF16)
        g_mlp = norm_mlp[l][None, :]
        g_fin = norm_final[None, :]
        last = l == depth - 1

        qab, kab, vtb, kat, vat, kit, qkv, z, qib, ki2b, gates = _inproj(yp, g_mix, w_pj, cfg["tm_p"], tp, True)
        oa = _dsa_prompt(qib, ki2b, gates, qab, kab, vtb, bp, tp, n_sel_p, cfg["tq"], cfg["ck"])
        ob, s_new = _gdn(qkv.reshape(bp, tp, CONV_CH), jnp.zeros((bp, CONV_W - 1, CONV_CH), F32),
                         gates.reshape(bp, tp, LANES), z.reshape(bp, tp, GDN_V),
                         jnp.zeros((bp, N_HEADS_GDN, HEAD_DIM_K, HEAD_DIM_V), F32),
                         conv_w[l], alog_row, dtb_row, gn, CHUNK, cfg["bb_p"])
        outs["kp"].append(jnp.transpose(kat, (0, 3, 1, 2)))
        outs["vp"].append(jnp.transpose(vat, (0, 3, 1, 2)))
        outs["kip"].append(jnp.transpose(kit, (0, 2, 1)))
        outs["sp"].append(s_new.astype(state_ssm.dtype))
        outs["cp"].append(qkv.reshape(bp, tp, CONV_CH)[:, tp - (CONV_W - 1):, :])
        yp = _mlp_layer(yp, oa, ob.reshape(bp * tp, GDN_V), w_out_b, g_mlp, w_up_b, w_dn_b, g_fin, last, cfg["tm_p"])

        qab, ka, va, ki, qkv, z, qib, gates = _inproj(ys, g_mix, w_pj, cfg["tm_s"], ts, False)
        qi_rows = qib.reshape(bs, ts, N_IDX_HEADS, IDX_DIM).transpose(0, 2, 1, 3).reshape(bs, N_IDX_HEADS * ts, IDX_DIM)
        wcol = gates[:, GATE_WI:GATE_WI + N_IDX_HEADS].reshape(bs, ts, N_IDX_HEADS)
        wcol = wcol.transpose(0, 2, 1).reshape(bs, N_IDX_HEADS * ts, 1)
        oa = _dsa_sample(l, page_table, qi_rows, wcol, ki.reshape(bs, ts, IDX_DIM), qab.reshape(bs, ts, ATT_W),
                         ka.reshape(bs, ts, ATT_W), va.reshape(bs, ts, ATT_W),
                         cache_ki_t, cache_k_t, cache_v_t, n_sel_s, cfg["pg"])
        pad_t = lambda a: jnp.pad(a.reshape(bs, ts, a.shape[-1]), ((0, 0), (0, CHUNK - ts), (0, 0)))
        ob, s_new = _gdn(pad_t(qkv), state_conv[l], pad_t(gates), pad_t(z), state_ssm[l].astype(F32),
                         conv_w[l], alog_row, dtb_row, gn, ts, cfg["bb_s"])
        outs["ks"].append(ka.reshape(bs, ts, N_HEADS_ATT, HEAD_DIM_ATT))
        outs["vs"].append(va.reshape(bs, ts, N_HEADS_ATT, HEAD_DIM_ATT))
        outs["kis"].append(ki.reshape(bs, ts, IDX_DIM))
        outs["ss"].append(s_new.astype(state_ssm.dtype))
        outs["cs"].append(qkv.reshape(bs, ts, CONV_CH)[:, ts - (CONV_W - 1):, :])
        ys = _mlp_layer(ys, oa.reshape(bs * ts, ATT_W).astype(BF16), ob[:, :ts, :].reshape(bs * ts, GDN_V),
                        w_out_b, g_mlp, w_up_b, w_dn_b, g_fin, last, cfg["tm_s"])

    st = lambda k: jnp.stack(outs[k])
    return (yp.reshape(bp, tp, d), ys.reshape(bs, ts, d),
            st("kp"), st("vp"), st("kip"), st("sp"), st("cp"),
            st("ks"), st("vs"), st("kis"), st("ss"), st("cs"))
```

```python
import functools

import numpy as np
import jax
import jax.numpy as jnp
from jax import lax
from jax.experimental import pallas as pl
from jax.experimental.pallas import tpu as pltpu

F32 = jnp.float32
BF16 = jnp.bfloat16
I32 = jnp.int32
HI = lax.Precision.HIGHEST

EPS = 1e-6
N_HEADS_ATT = 8
HEAD_DIM_ATT = 64
ATT_W = N_HEADS_ATT * HEAD_DIM_ATT
N_IDX_HEADS = 4
IDX_DIM = 64
IDX_W = N_IDX_HEADS * IDX_DIM
TOP_K_MAX = 256
N_HEADS_GDN = 4
HEAD_DIM_K = 128
HEAD_DIM_V = 128
GDN_K = N_HEADS_GDN * HEAD_DIM_K
GDN_V = N_HEADS_GDN * HEAD_DIM_V
CONV_W = 4
CONV_CH = 2 * GDN_K + GDN_V
CHUNK = 64

LANES = 128
SUBLANES = 8
VMEM_LIMIT = 56 * 1024 * 1024
N_ACC = 4

PJ_ATT = 0
PJ_CV = PJ_ATT + 3 * ATT_W
PJ_Z = PJ_CV + CONV_CH
PJ_QI = PJ_Z + GDN_V
PJ_KI = PJ_QI + IDX_W
PJ_GATE = PJ_KI + 2 * IDX_DIM
PJ_W = PJ_GATE + LANES
GATE_WI = 0
GATE_A = 4
GATE_B = 8

INT_MIN = np.int32(-2 ** 31)
KEY_NEG = np.int32(np.uint32(0x807FFFFF).view(np.int32))
NEG_INF = float("-inf")

NT_DIMS = (((1,), (1,)), ((), ()))
TN_DIMS = (((0,), (0,)), ((), ()))


def _key_to_float(key):
    bits = jnp.where(key < 0, key ^ np.int32(0x7FFFFFFF), key)
    return jnp.where(key < KEY_NEG, NEG_INF, lax.bitcast_convert_type(bits, F32))


def _sigmoid(x):
    return 1.0 / (1.0 + jnp.exp(-x))


def _softplus(x):
    return jnp.maximum(x, 0.0) + jnp.log1p(jnp.exp(-jnp.abs(x)))


def _rms(x, g):
    return x * lax.rsqrt(jnp.mean(x * x, axis=-1, keepdims=True) + EPS) * g


def _const_spec(shape):
    return pl.BlockSpec(shape, lambda *_: (0,) * len(shape), pipeline_mode=pl.Buffered(1))


def _params(n_grid_dims):
    return pltpu.CompilerParams(dimension_semantics=("arbitrary",) * n_grid_dims, vmem_limit_bytes=VMEM_LIMIT)


def _inproj_kernel(cache_t, x_ref, g_ref, w_ref, *out_refs):
    hb = _rms(x_ref[...], g_ref[...]).astype(BF16)
    proj = jnp.dot(hb, w_ref[...], preferred_element_type=F32)
    qa = proj[:, PJ_ATT:PJ_ATT + ATT_W]
    ka = proj[:, PJ_ATT + ATT_W:PJ_ATT + 2 * ATT_W]
    va = proj[:, PJ_ATT + 2 * ATT_W:PJ_ATT + 3 * ATT_W]
    qkv = proj[:, PJ_CV:PJ_CV + CONV_CH]
    z = proj[:, PJ_Z:PJ_Z + GDN_V]
    qi = proj[:, PJ_QI:PJ_QI + IDX_W]
    ki2 = proj[:, PJ_KI:PJ_KI + 2 * IDX_DIM]
    gates = proj[:, PJ_GATE:PJ_GATE + LANES]
    if cache_t:
        (qab_ref, kab_ref, vtb_ref, kat_ref, vat_ref, kit_ref, qkv_ref, z_ref, qib_ref, ki2b_ref, gate_ref) = out_refs
        tm = ka.shape[0]
        kab_ref[...] = ka.astype(BF16)
        ki2b_ref[...] = ki2.astype(BF16)
        vat = va.T
        vtb_ref[...] = vat.astype(BF16)
        kat_ref[...] = ka.T.reshape(N_HEADS_ATT, HEAD_DIM_ATT, tm)
        vat_ref[...] = vat.reshape(N_HEADS_ATT, HEAD_DIM_ATT, tm)
        kit_ref[...] = ki2.T[:IDX_DIM, :]
    else:
        (qab_ref, ka_ref, va_ref, ki_ref, qkv_ref, z_ref, qib_ref, gate_ref) = out_refs
        ka_ref[...] = ka
        va_ref[...] = va
        ki_ref[...] = ki2[:, :IDX_DIM]
    qab_ref[...] = qa.astype(BF16)
    qkv_ref[...] = qkv
    z_ref[...] = z
    qib_ref[...] = qi.astype(BF16)
    gate_ref[...] = gates


def _inproj(x2d, g, w, tm, seq_len, cache_t):
    n, d = x2d.shape
    row = lambda w_: pl.BlockSpec((tm, w_), lambda i: (i, 0))
    sds = jax.ShapeDtypeStruct
    common = [(row(CONV_CH), sds((n, CONV_CH), F32)), (row(GDN_V), sds((n, GDN_V), F32)),
              (row(IDX_W), sds((n, IDX_W), BF16)), (row(LANES), sds((n, LANES), F32))]
    if cache_t:
        assert seq_len % tm == 0
        n_t = seq_len // tm
        b = n // seq_len
        tok = lambda *lead: pl.BlockSpec((None,) + lead + (tm,), lambda i: (i // n_t,) + (0,) * len(lead) + (i % n_t,))
        head_t = (N_HEADS_ATT, HEAD_DIM_ATT)
        out = [(row(ATT_W), sds((n, ATT_W), BF16)),
               (row(ATT_W), sds((n, ATT_W), BF16)),
               (pl.BlockSpec((None, None, ATT_W, tm), lambda i: (i // n_t, i % n_t, 0, 0)),
                sds((b, n_t, ATT_W, tm), BF16)),
               (tok(*head_t), sds((b,) + head_t + (seq_len,), F32)),
               (tok(*head_t), sds((b,) + head_t + (seq_len,), F32)),
               (tok(IDX_DIM), sds((b, IDX_DIM, seq_len), F32)),
               common[0], common[1], common[2],
               (row(2 * IDX_DIM), sds((n, 2 * IDX_DIM), BF16)),
               common[3]]
    else:
        out = [(row(ATT_W), sds((n, ATT_W), BF16)), (row(ATT_W), sds((n, ATT_W), F32)),
               (row(ATT_W), sds((n, ATT_W), F32)), (row(IDX_DIM), sds((n, IDX_DIM), F32))] + common
    return pl.pallas_call(
        functools.partial(_inproj_kernel, cache_t),
        grid=(n // tm,),
        in_specs=[row(d), _const_spec((1, d)), _const_spec(w.shape)],
        out_specs=tuple(o[0] for o in out),
        out_shape=tuple(o[1] for o in out),
        compiler_params=_params(1),
        name="inproj",
    )(x2d, g, w)


def _mlp_kernel(ff_slab, final_norm, x_ref, oa_ref, ob_ref, wout_ref, gm_ref, wup_ref, wdn_ref, gf_ref, y_ref):
    mix = jnp.concatenate([oa_ref[...], ob_ref[...]], axis=1)
    x1 = x_ref[...] + jnp.dot(mix, wout_ref[...], preferred_element_type=F32)
    hm = _rms(x1, gm_ref[...]).astype(BF16)
    acc = x1
    d_ff = wup_ref.shape[1]
    for s in range(0, d_ff, ff_slab):
        up = jnp.dot(hm, wup_ref[:, s:s + ff_slab], preferred_element_type=F32)
        act = jnp.square(jnp.maximum(up, 0.0)).astype(BF16)
        acc = acc + jnp.dot(act, wdn_ref[s:s + ff_slab, :], preferred_element_type=F32)
    y_ref[...] = _rms(acc, gf_ref[...]) if final_norm else acc


def _mlp_layer(x2d, oa, ob, w_out, g_mlp, w_up, w_dn, g_final, final_norm, tm):
    n, d = x2d.shape
    row = lambda w: pl.BlockSpec((tm, w), lambda i: (i, 0))
    return pl.pallas_call(
        functools.partial(_mlp_kernel, 1024, final_norm),
        grid=(n // tm,),
        in_specs=[row(d), row(ATT_W), row(GDN_V), _const_spec(w_out.shape), _const_spec((1, d)),
                  _const_spec(w_up.shape), _const_spec(w_dn.shape), _const_spec((1, d))],
        out_specs=row(d),
        out_shape=jax.ShapeDtypeStruct((n, d), F32),
        compiler_params=_params(1),
        name="outproj_mlp",
    )(x2d, oa, ob, w_out, g_mlp, w_up, w_dn, g_final)


def _gdn_kernel(t_valid, bb, qkv_ref, conv0_ref, gate_ref, z_ref, s0_ref, cw_ref, alog_ref, dtb_ref, gn_ref,
                ob_ref, sout_ref, ext_ref, s_ref):
    c = pl.program_id(1)
    n_c = pl.num_programs(1)
    C = CHUNK
    HIST = SUBLANES

    @pl.when(c == 0)
    def _():
        for b in range(bb):
            ext_ref[b, 0:HIST, :] = jnp.zeros((HIST, CONV_CH), F32)
            ext_ref[b, HIST - (CONV_W - 1):HIST, :] = conv0_ref[b]
        s_ref[...] = s0_ref[...]

    H = N_HEADS_GDN
    R = H * C
    rows = lax.broadcasted_iota(I32, (R, R), 0)
    cols = lax.broadcasted_iota(I32, (R, R), 1)
    same_head = (rows // C) == (cols // C)
    incl = jnp.logical_and(same_head, rows >= cols)
    strict = jnp.logical_and(same_head, rows > cols)
    l_incl = (lax.broadcasted_iota(I32, (C, C), 0) >= lax.broadcasted_iota(I32, (C, C), 1)).astype(F32)
    gn = gn_ref[...]
    n_double = int(np.log2(C)) - 1
    stack = lambda f: jnp.concatenate([f(h) for h in range(H)], axis=0)
    head = lambda x, h: x[h * C:(h + 1) * C]

    for b in range(bb):
        ext_ref[b, HIST:HIST + C, :] = qkv_ref[b]
        conv = ext_ref[b, HIST - 3:HIST - 3 + C, :] * cw_ref[0:1, :]
        for j in range(1, CONV_W):
            conv = conv + ext_ref[b, HIST - 3 + j:HIST - 3 + j + C, :] * cw_ref[j:j + 1, :]
        conv = conv * _sigmoid(conv)
        ext_ref[b, 0:HIST, :] = ext_ref[b, C:C + HIST, :]

        gates = gate_ref[b]
        g_slab = -jnp.exp(alog_ref[...]) * _softplus(gates + dtb_ref[...])
        beta_slab = _sigmoid(gates)
        if t_valid < C:
            live = lax.broadcasted_iota(I32, (C, LANES), 0) < t_valid
            g_slab = jnp.where(live, g_slab, 0.0)
            beta_slab = jnp.where(live, beta_slab, 0.0)
        gc_slab = jnp.dot(l_incl, g_slab, precision=HI, preferred_element_type=F32)
        gc_rows = gc_slab.T
        z = z_ref[b]

        q = stack(lambda h: conv[:, h * HEAD_DIM_K:(h + 1) * HEAD_DIM_K])
        k = stack(lambda h: conv[:, GDN_K + h * HEAD_DIM_K:GDN_K + (h + 1) * HEAD_DIM_K])
        v = stack(lambda h: conv[:, 2 * GDN_K + h * HEAD_DIM_V:2 * GDN_K + (h + 1) * HEAD_DIM_V])
        zz = stack(lambda h: z[:, h * HEAD_DIM_V:(h + 1) * HEAD_DIM_V])
        q = q * lax.rsqrt(jnp.sum(q * q, axis=-1, keepdims=True) + EPS) * (HEAD_DIM_K ** -0.5)
        k = k * lax.rsqrt(jnp.sum(k * k, axis=-1, keepdims=True) + EPS)
        gc = stack(lambda h: gc_slab[:, GATE_A + h:GATE_A + h + 1])
        beta = stack(lambda h: beta_slab[:, GATE_B + h:GATE_B + h + 1])
        gc_row = jnp.concatenate([gc_rows[GATE_A + h:GATE_A + h + 1, :] for h in range(H)], axis=1)
        g_last = stack(lambda h: jnp.broadcast_to(gc_slab[C - 1:C, GATE_A + h:GATE_A + h + 1], (C, 1)))
        eg = jnp.exp(gc)
        kb = k * beta
        vb = v * beta
        decay = jnp.exp(jnp.where(incl, gc - gc_row, NEG_INF))
        kk = lax.dot_general(kb, k, NT_DIMS, preferred_element_type=F32)
        qk = lax.dot_general(q, k, NT_DIMS, preferred_element_type=F32)
        x = -jnp.where(strict, kk * decay, 0.0)
        aqk = jnp.where(incl, qk * decay, 0.0)
        r = jnp.concatenate([vb, kb * eg], axis=1)
        r = r + jnp.dot(x, r, precision=HI, preferred_element_type=F32)
        pw = x
        for _ in range(n_double):
            pw = jnp.dot(pw, pw, preferred_element_type=F32)
            r = r + jnp.dot(pw, r, preferred_element_type=F32)
        u = r[:, :HEAD_DIM_V]
        w = r[:, HEAD_DIM_V:]
        s_old = [s_ref[b, h] for h in range(H)]
        v_new = u - stack(lambda h: jnp.dot(head(w, h), s_old[h], preferred_element_type=F32))
        o = (stack(lambda h: jnp.dot(head(q * eg, h), s_old[h], preferred_element_type=F32))
             + jnp.dot(aqk, v_new, preferred_element_type=F32))
        kd = k * jnp.exp(g_last - gc)
        for h in range(H):
            s_ref[b, h] = (s_old[h] * jnp.exp(g_last[h * C:h * C + 1, :])
                           + lax.dot_general(head(kd, h), head(v_new, h), TN_DIMS, preferred_element_type=F32))
        o = (_rms(o, gn) * (zz * _sigmoid(zz))).astype(ob_ref.dtype)
        for h in range(H):
            ob_ref[b, :, h * HEAD_DIM_V:(h + 1) * HEAD_DIM_V] = head(o, h)

    @pl.when(c == n_c - 1)
    def _():
        sout_ref[...] = s_ref[...]


def _gdn(qkv, conv0, gates, z, s0, conv_w, alog_row, dtb_row, gn, t_valid, bb):
    b, t, _ = qkv.shape
    n_c = t // CHUNK
    blk = lambda w: pl.BlockSpec((bb, CHUNK, w), lambda i, c: (i, c, 0))
    per_b = lambda shape: pl.BlockSpec((bb,) + shape, lambda i, c: (i,) + (0,) * len(shape))
    state = (N_HEADS_GDN, HEAD_DIM_K, HEAD_DIM_V)
    return pl.pallas_call(
        functools.partial(_gdn_kernel, t_valid, bb),
        grid=(b // bb, n_c),
        in_specs=[blk(CONV_CH), per_b((CONV_W - 1, CONV_CH)), blk(LANES), blk(GDN_V), per_b(state),
                  _const_spec((CONV_W, CONV_CH)), _const_spec((1, LANES)), _const_spec((1, LANES)),
                  _const_spec((1, HEAD_DIM_V))],
        out_specs=(blk(GDN_V), per_b(state)),
        out_shape=(jax.ShapeDtypeStruct((b, t, GDN_V), BF16), jax.ShapeDtypeStruct((b,) + state, F32)),
        scratch_shapes=[pltpu.VMEM((bb, CHUNK + 2 * SUBLANES, CONV_CH), F32), pltpu.VMEM((bb,) + state, F32)],
        compiler_params=_params(2),
        name="gdn",
    )(qkv, conv0, gates, z, s0, conv_w, alog_row, dtb_row, gn)


def _topk_rule(count_ge, count_tie_le, qshape, n_sel, n_idx_bits, j_scr):
    def bis(it, u):
        cu = u | lax.shift_left(jnp.int32(1), 31 - it)
        return jnp.where(count_ge(_key_to_float(cu ^ INT_MIN)) >= n_sel, cu, u)

    thr_key = lax.fori_loop(0, 32, bis, jnp.zeros(qshape, I32)) ^ INT_MIN
    thr = _key_to_float(thr_key)
    nxt = _key_to_float(thr_key + 1)
    need = n_sel - count_ge(nxt)
    excess = jnp.logical_and(count_ge(thr) > n_sel, thr_key > KEY_NEG)
    j_scr[...] = jnp.full(qshape, np.int32(2 ** 30), I32)

    @pl.when(jnp.max(excess.astype(I32)) > 0)
    def _():
        def jbis(it, j):
            cj = j | lax.shift_left(jnp.int32(1), n_idx_bits - 1 - it)
            return jnp.where(count_tie_le(thr, nxt, cj - 1) < need, cj, j)

        j_scr[...] = lax.fori_loop(0, n_idx_bits, jbis, jnp.zeros(qshape, I32))

    return thr, nxt, j_scr[...]


def _dsa_prompt_kernel(n_sel, tq, ck, t_total, qi_ref, ki2_ref, gate_ref, q_ref, k_ref, vt_ref,
                       o_ref, sc_scr, s_scr_even, s_scr_odd, j_scr):
    s_scrs = (s_scr_even, s_scr_odd)
    i = pl.program_id(1)
    n_chunks = lax.div(i * tq + tq + ck - 1, ck)
    grp = ck // SUBLANES
    g3 = (grp, SUBLANES, tq)
    q_id = i * tq + lax.broadcasted_iota(I32, g3, 2)
    k_in_chunk = lax.broadcasted_iota(I32, g3, 0) * SUBLANES + lax.broadcasted_iota(I32, g3, 1)
    lane_lo = lax.broadcasted_iota(I32, (tq, LANES), 1) < HEAD_DIM_ATT
    rep8 = lambda v: jnp.broadcast_to(v, (SUBLANES, tq))

    qi = qi_ref[...] * (IDX_DIM ** -0.5)
    wi_t = gate_ref[...].T[GATE_WI:GATE_WI + N_IDX_HEADS, :] * (N_IDX_HEADS ** -0.5)
    qi_heads = []
    for h in range(N_IDX_HEADS):
        pair = qi[:, (h // 2) * LANES:(h // 2 + 1) * LANES]
        keep = lane_lo if h % 2 == 0 else jnp.logical_not(lane_lo)
        qi_heads.append(jnp.where(keep, pair, jnp.zeros_like(pair)))

    def score_body(c, carry):
        k0 = pl.multiple_of(c * ck, ck)
        kic = ki2_ref[pl.ds(k0, ck), :]
        sc = jnp.zeros((ck, tq), F32)
        for h in range(N_IDX_HEADS):
            logit = lax.dot_general(kic, qi_heads[h], NT_DIMS, preferred_element_type=F32)
            sc = sc + jnp.maximum(logit, 0.0) * wi_t[h:h + 1, :]
        causal = k0 + k_in_chunk <= q_id
        sc_scr[c] = jnp.where(causal, sc.reshape(g3), NEG_INF).reshape(ck, tq)
        return carry

    lax.fori_loop(0, n_chunks, score_body, 0)

    acc_shape = (N_ACC, SUBLANES, tq)
    fold = lambda x, op: op(x.reshape(grp // N_ACC, N_ACC, SUBLANES, tq), axis=0)
    finish = lambda acc, op: op(op(acc, axis=0), axis=0, keepdims=True)

    def count_ge(cand):
        cand8 = rep8(cand)
        body = lambda c, acc: acc + fold(jnp.where(sc_scr[c].reshape(g3) >= cand8, 1.0, 0.0), jnp.sum)
        return finish(lax.fori_loop(0, n_chunks, body, jnp.zeros(acc_shape, F32)), jnp.sum)

    def count_tie_le(thr, nxt, j):
        thr8, nxt8, j8 = rep8(thr), rep8(nxt), rep8(j)

        def body(c, acc):
            s = sc_scr[c].reshape(g3)
            hit = jnp.logical_and(jnp.logical_and(s >= thr8, jnp.logical_not(s >= nxt8)), c * ck + k_in_chunk <= j8)
            return acc + fold(jnp.where(hit, 1.0, 0.0), jnp.sum)

        return finish(lax.fori_loop(0, n_chunks, body, jnp.zeros(acc_shape, F32)), jnp.sum)

    thr, nxt, jmax = _topk_rule(count_ge, count_tie_le, (1, tq), n_sel, int(np.ceil(np.log2(t_total))), j_scr)
    thr8, nxt8, j8 = rep8(thr), rep8(nxt), rep8(jmax)

    def mask_body(c, carry):
        s = sc_scr[c].reshape(g3)
        sel = jnp.logical_or(s >= nxt8, jnp.logical_and(s >= thr8, c * ck + k_in_chunk <= j8))
        sel = jnp.logical_and(sel, s > NEG_INF)
        sc_scr[c] = jnp.where(sel, 0.0, NEG_INF).reshape(ck, tq)
        return carry

    lax.fori_loop(0, n_chunks, mask_body, 0)

    def q_masked(h):
        qp = q_ref[:, (h // 2) * LANES:(h // 2 + 1) * LANES] * (HEAD_DIM_ATT ** -0.5)
        keep = lane_lo if h % 2 == 0 else jnp.logical_not(lane_lo)
        return jnp.where(keep, qp, jnp.zeros_like(qp))

    sub = LANES
    fold_sub = lambda x, op: op(x.reshape(sub // (N_ACC * SUBLANES), N_ACC, SUBLANES, tq), axis=0)

    def s_pass(h, qm, c, macc):
        k0 = pl.multiple_of(c * ck, ck)
        for r in range(0, ck, sub):
            kc = k_ref[pl.ds(k0 + r, sub), (h // 2) * LANES:(h // 2 + 1) * LANES]
            st = lax.dot_general(kc, qm, NT_DIMS, preferred_element_type=F32) + sc_scr[c, r:r + sub, :]
            s_scrs[h % 2][c, r:r + sub, :] = st
            macc = jnp.maximum(macc, fold_sub(st, jnp.max))
        return macc

    def pv_pass(h, m, c, lacc, oacc):
        for r in range(0, ck, sub):
            pt = jnp.exp(s_scrs[h % 2][c, r:r + sub, :] - m)
            vt = vt_ref[c, h * HEAD_DIM_ATT:(h + 1) * HEAD_DIM_ATT, r:r + sub]
            lacc = lacc + fold_sub(pt, jnp.sum)
            oacc = oacc + jnp.dot(vt, pt.astype(BF16), preferred_element_type=F32)
        return lacc, oacc

    neg = jnp.full(acc_shape, NEG_INF, F32)
    zero_l = jnp.zeros(acc_shape, F32)
    zero_o = jnp.zeros((HEAD_DIM_ATT, tq), F32)
    qm0 = q_masked(0)
    macc = lax.fori_loop(0, n_chunks, lambda c, a: s_pass(0, qm0, c, a), neg)
    o_rows = []
    for h in range(N_HEADS_ATT):
        m = finish(macc, jnp.max)
        if h + 1 < N_HEADS_ATT:
            qm_next = q_masked(h + 1)

            def body(c, carry, h=h, m=m, qm_next=qm_next):
                lacc, oacc, macc_next = carry
                macc_next = s_pass(h + 1, qm_next, c, macc_next)
                lacc, oacc = pv_pass(h, m, c, lacc, oacc)
                return lacc, oacc, macc_next

            lacc, oacc, macc = lax.fori_loop(0, n_chunks, body, (zero_l, zero_o, neg))
        else:
            lacc, oacc = lax.fori_loop(0, n_chunks, lambda c, cr, h=h, m=m: pv_pass(h, m, c, *cr), (zero_l, zero_o))
        o_rows.append(oacc / finish(lacc, jnp.sum))
    o_ref[...] = jnp.concatenate(o_rows, axis=0).T.astype(o_ref.dtype)


def _dsa_prompt(qib, ki2b, gates, qab, kab, vtb, batch, t, n_sel, tq, ck):
    n_q = t // tq
    tile = lambda w: pl.BlockSpec((tq, w), lambda b, i: (b * n_q + i, 0))
    seq = lambda w: pl.BlockSpec((t, w), lambda b, i: (b, 0))
    return pl.pallas_call(
        functools.partial(_dsa_prompt_kernel, n_sel, tq, ck, t),
        grid=(batch, n_q),
        in_specs=[tile(IDX_W), seq(2 * IDX_DIM), tile(LANES), tile(ATT_W), seq(ATT_W),
                  pl.BlockSpec((None, t // ck, ATT_W, ck), lambda b, i: (b, 0, 0, 0))],
        out_specs=tile(ATT_W),
        out_shape=jax.ShapeDtypeStruct((batch * t, ATT_W), BF16),
        scratch_shapes=[pltpu.VMEM((t // ck, ck, tq), F32), pltpu.VMEM((t // ck, ck, tq), F32),
                        pltpu.VMEM((t // ck, ck, tq), F32), pltpu.VMEM((1, tq), I32)],
        compiler_params=_params(2),
        name="dsa_prompt",
    )(qib, ki2b, gates, qab, kab, vtb)


def _dsa_sample_kernel(n_sel, n_pages, pg, pt_ref, qi_ref, wcol_ref, kinew_ref, q_ref, knew_ref, vnew_ref, *refs):
    ki_refs, k_refs, v_refs = refs[:pg], refs[pg:2 * pg], refs[2 * pg:3 * pg]
    o_ref = refs[3 * pg]
    sc_scr, s_scr, macc_scr, lacc_scr, oacc_scr, j_scr = refs[3 * pg + 1:]
    del pt_ref
    s = pl.program_id(1)
    nst = n_pages // pg
    n_ch = n_pages + 1
    tq = SUBLANES
    page = LANES
    qshape = (tq, page)
    lane = lax.broadcasted_iota(I32, qshape, 1)

    def new_t(x):
        wdt = x.shape[1]
        wp = max(wdt, page)
        if wp > wdt:
            x = jnp.concatenate([x, jnp.zeros((tq, wp - wdt), F32)], axis=1)
        sq = jnp.concatenate([x, jnp.zeros((page - tq, wp), F32)], axis=0)
        return jnp.concatenate([sq[:, t * page:(t + 1) * page].T for t in range(wp // page)], axis=0)[:wdt]

    @pl.when(s < nst)
    def _():
        qi = qi_ref[...] * (IDX_DIM ** -0.5)
        wcol = wcol_ref[...] * (N_IDX_HEADS ** -0.5)

        def score_chunk(ki_t):
            logit = jnp.dot(qi, ki_t.astype(BF16), preferred_element_type=F32)
            lw = jnp.maximum(logit, 0.0) * wcol
            sc = lw[0:tq]
            for h in range(1, N_IDX_HEADS):
                sc = sc + lw[h * tq:(h + 1) * tq]
            return sc

        for p in range(pg):
            sc_scr[s * pg + p] = score_chunk(ki_refs[p][...])

        @pl.when(s == 0)
        def _():
            sc = score_chunk(new_t(kinew_ref[...]))
            causal = lane <= lax.broadcasted_iota(I32, qshape, 0)
            sc_scr[n_pages] = jnp.where(causal, sc, NEG_INF)

        @pl.when(s == nst - 1)
        def _():
            def row_total(hits):
                accs = [jnp.zeros(qshape, F32)] * N_ACC
                for c in range(n_ch):
                    accs[c % N_ACC] = accs[c % N_ACC] + jnp.where(hits(c), 1.0, 0.0)
                tot = functools.reduce(lambda a, b: a + b, accs)
                return jnp.broadcast_to(jnp.sum(tot, axis=1, keepdims=True), qshape)

            def count_ge(cand):
                return row_total(lambda c: sc_scr[c] >= cand)

            def tie(sc, thr, nxt, c, j):
                return jnp.logical_and(jnp.logical_and(sc >= thr, jnp.logical_not(sc >= nxt)), c * page + lane <= j)

            def count_tie_le(thr, nxt, j):
                return row_total(lambda c: tie(sc_scr[c], thr, nxt, c, j))

            thr, nxt, jmax = _topk_rule(count_ge, count_tie_le, qshape, n_sel,
                                        int(np.ceil(np.log2(n_ch * page))), j_scr)

            def bias_body(c, carry):
                sc = sc_scr[c]
                sel = jnp.logical_or(sc >= nxt, tie(sc, thr, nxt, c, jmax))
                sel = jnp.logical_and(sel, sc > NEG_INF)
                sc_scr[c] = jnp.where(sel, 0.0, NEG_INF)
                return carry

            lax.fori_loop(0, n_ch, bias_body, 0)

    @pl.when(jnp.logical_and(s >= nst, s < 2 * nst))
    def _():
        j = s - nst
        q = q_ref[...] * (HEAD_DIM_ATT ** -0.5)

        def s_chunk(kt_of_head, c):
            bias = sc_scr[c]
            parts = []
            for h in range(N_HEADS_ATT):
                qh = q[:, h * HEAD_DIM_ATT:(h + 1) * HEAD_DIM_ATT]
                parts.append(jnp.dot(qh, kt_of_head(h).astype(BF16), preferred_element_type=F32) + bias)
            sc = jnp.concatenate(parts, axis=0)
            s_scr[c] = sc
            return sc

        @pl.when(j == 0)
        def _():
            knew_t = new_t(knew_ref[...])
            macc_scr[...] = s_chunk(lambda h: knew_t[h * HEAD_DIM_ATT:(h + 1) * HEAD_DIM_ATT], n_pages)

        macc = macc_scr[...]
        for p in range(pg):
            macc = jnp.maximum(macc, s_chunk(lambda h, p=p: k_refs[p][h], j * pg + p))
        macc_scr[...] = macc

    @pl.when(s >= 2 * nst)
    def _():
        j = s - 2 * nst
        m = jnp.max(macc_scr[...], axis=1, keepdims=True)

        def pv_chunk(vt_of_head, c):
            pch = jnp.exp(s_scr[c] - m)
            pb = pch.astype(BF16)
            outs = [lax.dot_general(pb[h * tq:(h + 1) * tq], vt_of_head(h).astype(BF16), NT_DIMS,
                                    preferred_element_type=F32) for h in range(N_HEADS_ATT)]
            return pch, jnp.concatenate(outs, axis=0)

        @pl.when(j == 0)
        def _():
            vnew_t = new_t(vnew_ref[...])
            l0, o0 = pv_chunk(lambda h: vnew_t[h * HEAD_DIM_ATT:(h + 1) * HEAD_DIM_ATT], n_pages)
            lacc_scr[...] = l0
            oacc_scr[...] = o0

        lacc = lacc_scr[...]
        oacc = oacc_scr[...]
        for p in range(pg):
            lp, op = pv_chunk(lambda h, p=p: v_refs[p][h], j * pg + p)
            lacc = lacc + lp
            oacc = oacc + op
        lacc_scr[...] = lacc
        oacc_scr[...] = oacc

        @pl.when(j == nst - 1)
        def _():
            o = oacc / jnp.sum(lacc, axis=1, keepdims=True)
            o_ref[...] = jnp.concatenate([o[h * tq:(h + 1) * tq] for h in range(N_HEADS_ATT)], axis=1)


def _dsa_sample(layer, page_table, qi, wcol, ki_new, q, k_new, v_new, cache_ki_t, cache_k_t, cache_v_t, n_sel, pg):
    b, n_pages = page_table.shape
    page = cache_k_t.shape[-1]
    assert page == LANES and n_pages % pg == 0 and ki_new.shape[1] == SUBLANES
    nst = n_pages // pg
    per_b = lambda shape: pl.BlockSpec((None,) + shape, lambda i, s, pt: (i,) + (0,) * len(shape))

    def page_spec(lead, phase, p):
        def idx(i, s, pt):
            step = jnp.clip(s - phase * nst, 0, nst - 1)
            return (layer, pt[i, step * pg + p]) + (0,) * (len(lead) + 1)
        return pl.BlockSpec((None, None) + lead + (page,), idx)

    hq_idx = N_IDX_HEADS * SUBLANES
    in_specs = [per_b((hq_idx, IDX_DIM)), per_b((hq_idx, 1)), per_b((SUBLANES, IDX_DIM)), per_b((SUBLANES, ATT_W)),
                per_b((SUBLANES, ATT_W)), per_b((SUBLANES, ATT_W))]
    in_specs += [page_spec((IDX_DIM,), 0, p) for p in range(pg)]
    in_specs += [page_spec((N_HEADS_ATT, HEAD_DIM_ATT), 1, p) for p in range(pg)]
    in_specs += [page_spec((N_HEADS_ATT, HEAD_DIM_ATT), 2, p) for p in range(pg)]
    hq = N_HEADS_ATT * SUBLANES
    grid_spec = pltpu.PrefetchScalarGridSpec(
        num_scalar_prefetch=1,
        grid=(b, 3 * nst),
        in_specs=in_specs,
        out_specs=per_b((SUBLANES, ATT_W)),
        scratch_shapes=[pltpu.VMEM((n_pages + 1, SUBLANES, LANES), F32), pltpu.VMEM((n_pages + 1, hq, LANES), F32),
                        pltpu.VMEM((hq, LANES), F32), pltpu.VMEM((hq, LANES), F32),
                        pltpu.VMEM((hq, HEAD_DIM_ATT), F32), pltpu.VMEM((SUBLANES, LANES), I32)],
    )
    return pl.pallas_call(
        functools.partial(_dsa_sample_kernel, n_sel, n_pages, pg),
        grid_spec=grid_spec,
        out_shape=jax.ShapeDtypeStruct((b, SUBLANES, ATT_W), F32),
        compiler_params=_params(2),
        name="dsa_sample",
    )(page_table, qi, wcol, ki_new, q, k_new, v_new,
      *([cache_ki_t] * pg), *([cache_k_t] * pg), *([cache_v_t] * pg))


def _regroup_w_in(w):
    o_qi = 3 * ATT_W
    o_ki = o_qi + IDX_W
    o_wi = o_ki + IDX_DIM
    o_cv = o_wi + N_IDX_HEADS
    o_a = o_cv + CONV_CH
    o_b = o_a + N_HEADS_GDN
    o_z = o_b + N_HEADS_GDN
    gate_cols = jnp.concatenate([w[:, o_wi:o_cv], w[:, o_a:o_b], w[:, o_b:o_z]], axis=1)
    gate_cols = jnp.pad(gate_cols, ((0, 0), (0, LANES - gate_cols.shape[1])))
    out = jnp.concatenate([w[:, :o_qi], w[:, o_cv:o_a], w[:, o_z:o_z + GDN_V], w[:, o_qi:o_ki],
                           w[:, o_ki:o_wi], w[:, o_ki:o_wi], gate_cols], axis=1)
    assert out.shape[1] == PJ_W
    return out.astype(BF16)


def _gate_row(v):
    return jnp.pad(v.astype(F32), (GATE_A, LANES - GATE_A - v.shape[0]))[None, :]


def _tiles(bp, tp, bs, ts, n_pages):
    ck = min(512, tp)
    return dict(
        tm_p=ck,
        tm_s=min(256, bs * ts),
        tq=min(256, tp),
        ck=ck,
        bb_p=min(4, bp),
        bb_s=min(4, bs),
        pg=min(16, n_pages),
    )


def kernel(x_prompt, x_sample, cache_k, cache_v, cache_kidx, state_ssm, state_conv, page_table, norm_mix, w_in,
           conv_w, a_log, dt_bias, norm_gdn_out, w_out, norm_mlp, w_up, w_down, norm_final):
    depth = w_in.shape[0]
    bp, tp, d = x_prompt.shape
    bs, ts, _ = x_sample.shape
    page = cache_k.shape[2]
    past = page_table.shape[1] * page
    cfg = _tiles(bp, tp, bs, ts, page_table.shape[1])
    n_sel_p = min(TOP_K_MAX, tp // 4)
    n_sel_s = min(TOP_K_MAX, (past + ts) // 4)
    assert tp % CHUNK == 0 and CONV_W - 1 <= ts <= CHUNK
    assert cfg["tq"] >= n_sel_p and tp % cfg["tq"] == 0 and tp % cfg["ck"] == 0 and cfg["ck"] % cfg["tq"] == 0
    assert bp % cfg["bb_p"] == 0 and bs % cfg["bb_s"] == 0

    cache_k_t = jnp.transpose(cache_k, (0, 1, 3, 4, 2))
    cache_v_t = jnp.transpose(cache_v, (0, 1, 3, 4, 2))
    cache_ki_t = jnp.transpose(cache_kidx, (0, 1, 3, 2))

    yp = x_prompt.reshape(bp * tp, d)
    ys = x_sample.reshape(bs * ts, d)
    outs = {k: [] for k in ("kp", "vp", "kip", "sp", "cp", "ks", "vs", "kis", "ss", "cs")}
    for l in range(depth):
        w_pj = _regroup_w_in(w_in[l])
        g_mix = norm_mix[l][None, :]
        alog_row, dtb_row = _gate_row(a_log[l]), _gate_row(dt_bias[l])
        gn = norm_gdn_out[l][None, :]
        w_out_b, w_up_b, w_dn_b = w_out[l].astype(BF16), w_up[l].astype(BF16), w_down[l].astype(BF16)
        g_mlp = norm_mlp[l][None, :]
        g_fin = norm_final[None, :]
        last = l == depth - 1

        qab, kab, vtb, kat, vat, kit, qkv, z, qib, ki2b, gates = _inproj(yp, g_mix, w_pj, cfg["tm_p"], tp, True)
        oa = _dsa_prompt(qib, ki2b, gates, qab, kab, vtb, bp, tp, n_sel_p, cfg["tq"], cfg["ck"])
        ob, s_new = _gdn(qkv.reshape(bp, tp, CONV_CH), jnp.zeros((bp, CONV_W - 1, CONV_CH), F32),
                         gates.reshape(bp, tp, LANES), z.reshape(bp, tp, GDN_V),
                         jnp.zeros((bp, N_HEADS_GDN, HEAD_DIM_K, HEAD_DIM_V), F32),
                         conv_w[l], alog_row, dtb_row, gn, CHUNK, cfg["bb_p"])
        outs["kp"].append(jnp.transpose(kat, (0, 3, 1, 2)))
        outs["vp"].append(jnp.transpose(vat, (0, 3, 1, 2)))
        outs["kip"].append(jnp.transpose(kit, (0, 2, 1)))
        outs["sp"].append(s_new.astype(state_ssm.dtype))
        outs["cp"].append(qkv.reshape(bp, tp, CONV_CH)[:, tp - (CONV_W - 1):, :])
        yp = _mlp_layer(yp, oa, ob.reshape(bp * tp, GDN_V), w_out_b, g_mlp, w_up_b, w_dn_b, g_fin, last, cfg["tm_p"])

        qab, ka, va, ki, qkv, z, qib, gates = _inproj(ys, g_mix, w_pj, cfg["tm_s"], ts, False)
        qi_rows = qib.reshape(bs, ts, N_IDX_HEADS, IDX_DIM).transpose(0, 2, 1, 3).reshape(bs, N_IDX_HEADS * ts, IDX_DIM)
        wcol = gates[:, GATE_WI:GATE_WI + N_IDX_HEADS].reshape(bs, ts, N_IDX_HEADS)
        wcol = wcol.transpose(0, 2, 1).reshape(bs, N_IDX_HEADS * ts, 1)
        oa = _dsa_sample(l, page_table, qi_rows, wcol, ki.reshape(bs, ts, IDX_DIM), qab.reshape(bs, ts, ATT_W),
                         ka.reshape(bs, ts, ATT_W), va.reshape(bs, ts, ATT_W),
                         cache_ki_t, cache_k_t, cache_v_t, n_sel_s, cfg["pg"])
        pad_t = lambda a: jnp.pad(a.reshape(bs, ts, a.shape[-1]), ((0, 0), (0, CHUNK - ts), (0, 0)))
        ob, s_new = _gdn(pad_t(qkv), state_conv[l], pad_t(gates), pad_t(z), state_ssm[l].astype(F32),
                         conv_w[l], alog_row, dtb_row, gn, ts, cfg["bb_s"])
        outs["ks"].append(ka.reshape(bs, ts, N_HEADS_ATT, HEAD_DIM_ATT))
        outs["vs"].append(va.reshape(bs, ts, N_HEADS_ATT, HEAD_DIM_ATT))
        outs["kis"].append(ki.reshape(bs, ts, IDX_DIM))
        outs["ss"].append(s_new.astype(state_ssm.dtype))
        outs["cs"].append(qkv.reshape(bs, ts, CONV_CH)[:, ts - (CONV_W - 1):, :])
        ys = _mlp_layer(ys, oa.reshape(bs * ts, ATT_W).astype(BF16), ob[:, :ts, :].reshape(bs * ts, GDN_V),
                        w_out_b, g_mlp, w_up_b, w_dn_b, g_fin, last, cfg["tm_s"])

    st = lambda k: jnp.stack(outs[k])
    return (yp.reshape(bp, tp, d), ys.reshape(bs, ts, d),
            st("kp"), st("vp"), st("kip"), st("sp"), st("cp"),
            st("ks"), st("vs"), st("kis"), st("ss"), st("cs"))
```

```python
import functools

import numpy as np
import jax
import jax.numpy as jnp
from jax import lax
from jax.experimental import pallas as pl
from jax.experimental.pallas import tpu as pltpu

F32 = jnp.float32
BF16 = jnp.bfloat16
I32 = jnp.int32
HI = lax.Precision.HIGHEST

EPS = 1e-6
N_HEADS_ATT = 8
HEAD_DIM_ATT = 64
ATT_W = N_HEADS_ATT * HEAD_DIM_ATT
N_IDX_HEADS = 4
IDX_DIM = 64
IDX_W = N_IDX_HEADS * IDX_DIM
TOP_K_MAX = 256
N_HEADS_GDN = 4
HEAD_DIM_K = 128
HEAD_DIM_V = 128
GDN_K = N_HEADS_GDN * HEAD_DIM_K
GDN_V = N_HEADS_GDN * HEAD_DIM_V
CONV_W = 4
CONV_CH = 2 * GDN_K + GDN_V
CHUNK = 64

LANES = 128
SUBLANES = 8
VMEM_LIMIT = 56 * 1024 * 1024
N_ACC = 4

PJ_ATT = 0
PJ_CV = PJ_ATT + 3 * ATT_W
PJ_Z = PJ_CV + CONV_CH
PJ_QI = PJ_Z + GDN_V
PJ_KI = PJ_QI + IDX_W
PJ_GATE = PJ_KI + 2 * IDX_DIM
PJ_W = PJ_GATE + LANES
GATE_WI = 0
GATE_A = 4
GATE_B = 8

INT_MIN = np.int32(-2 ** 31)
KEY_NEG = np.int32(np.uint32(0x807FFFFF).view(np.int32))
NEG_INF = float("-inf")

NT_DIMS = (((1,), (1,)), ((), ()))
TN_DIMS = (((0,), (0,)), ((), ()))


def _key_to_float(key):
    bits = jnp.where(key < 0, key ^ np.int32(0x7FFFFFFF), key)
    return jnp.where(key < KEY_NEG, NEG_INF, lax.bitcast_convert_type(bits, F32))


def _sigmoid(x):
    return 1.0 / (1.0 + jnp.exp(-x))


def _softplus(x):
    return jnp.maximum(x, 0.0) + jnp.log1p(jnp.exp(-jnp.abs(x)))


def _split_bf16(x):
    hi = x.astype(BF16)
    return hi, (x - hi.astype(F32)).astype(BF16)


def _dot_split(a, b):
    dot = lambda p, q: jnp.dot(p, q, preferred_element_type=F32)
    return dot(a[0], b[0]) + (dot(a[0], b[1]) + dot(a[1], b[0]))


def _rms(x, g):
    return x * lax.rsqrt(jnp.mean(x * x, axis=-1, keepdims=True) + EPS) * g


def _const_spec(shape):
    return pl.BlockSpec(shape, lambda *_: (0,) * len(shape), pipeline_mode=pl.Buffered(1))


def _params(n_grid_dims):
    return pltpu.CompilerParams(dimension_semantics=("arbitrary",) * n_grid_dims, vmem_limit_bytes=VMEM_LIMIT)


def _inproj_kernel(cache_t, x_ref, g_ref, w_ref, *out_refs):
    hb = _rms(x_ref[...], g_ref[...]).astype(BF16)
    proj = jnp.dot(hb, w_ref[...], preferred_element_type=F32)
    qa = proj[:, PJ_ATT:PJ_ATT + ATT_W]
    ka = proj[:, PJ_ATT + ATT_W:PJ_ATT + 2 * ATT_W]
    va = proj[:, PJ_ATT + 2 * ATT_W:PJ_ATT + 3 * ATT_W]
    qkv = proj[:, PJ_CV:PJ_CV + CONV_CH]
    z = proj[:, PJ_Z:PJ_Z + GDN_V]
    qi = proj[:, PJ_QI:PJ_QI + IDX_W]
    ki2 = proj[:, PJ_KI:PJ_KI + 2 * IDX_DIM]
    gates = proj[:, PJ_GATE:PJ_GATE + LANES]
    if cache_t:
        (qab_ref, kab_ref, vtb_ref, kat_ref, vat_ref, kit_ref, qkv_ref, z_ref, qib_ref, ki2b_ref, gate_ref) = out_refs
        tm = ka.shape[0]
        kab_ref[...] = ka.astype(BF16)
        ki2b_ref[...] = ki2.astype(BF16)
        vat = va.T
        vtb_ref[...] = vat.astype(BF16)
        kat_ref[...] = ka.T.reshape(N_HEADS_ATT, HEAD_DIM_ATT, tm)
        vat_ref[...] = vat.reshape(N_HEADS_ATT, HEAD_DIM_ATT, tm)
        kit_ref[...] = ki2.T[:IDX_DIM, :]
    else:
        (qab_ref, ka_ref, va_ref, ki_ref, qkv_ref, z_ref, qib_ref, gate_ref) = out_refs
        ka_ref[...] = ka
        va_ref[...] = va
        ki_ref[...] = ki2[:, :IDX_DIM]
    qab_ref[...] = qa.astype(BF16)
    qkv_ref[...] = qkv
    z_ref[...] = z
    qib_ref[...] = qi.astype(BF16)
    gate_ref[...] = gates


def _inproj(x2d, g, w, tm, seq_len, cache_t):
    n, d = x2d.shape
    row = lambda w_: pl.BlockSpec((tm, w_), lambda i: (i, 0))
    sds = jax.ShapeDtypeStruct
    common = [(row(CONV_CH), sds((n, CONV_CH), F32)), (row(GDN_V), sds((n, GDN_V), F32)),
              (row(IDX_W), sds((n, IDX_W), BF16)), (row(LANES), sds((n, LANES), F32))]
    if cache_t:
        assert seq_len % tm == 0
        n_t = seq_len // tm
        b = n // seq_len
        tok = lambda *lead: pl.BlockSpec((None,) + lead + (tm,), lambda i: (i // n_t,) + (0,) * len(lead) + (i % n_t,))
        head_t = (N_HEADS_ATT, HEAD_DIM_ATT)
        out = [(row(ATT_W), sds((n, ATT_W), BF16)),
               (row(ATT_W), sds((n, ATT_W), BF16)),
               (pl.BlockSpec((None, None, ATT_W, tm), lambda i: (i // n_t, i % n_t, 0, 0)),
                sds((b, n_t, ATT_W, tm), BF16)),
               (tok(*head_t), sds((b,) + head_t + (seq_len,), F32)),
               (tok(*head_t), sds((b,) + head_t + (seq_len,), F32)),
               (tok(IDX_DIM), sds((b, IDX_DIM, seq_len), F32)),
               common[0], common[1], common[2],
               (row(2 * IDX_DIM), sds((n, 2 * IDX_DIM), BF16)),
               common[3]]
    else:
        out = [(row(ATT_W), sds((n, ATT_W), BF16)), (row(ATT_W), sds((n, ATT_W), F32)),
               (row(ATT_W), sds((n, ATT_W), F32)), (row(IDX_DIM), sds((n, IDX_DIM), F32))] + common
    return pl.pallas_call(
        functools.partial(_inproj_kernel, cache_t),
        grid=(n // tm,),
        in_specs=[row(d), _const_spec((1, d)), _const_spec(w.shape)],
        out_specs=tuple(o[0] for o in out),
        out_shape=tuple(o[1] for o in out),
        compiler_params=_params(1),
        name="inproj",
    )(x2d, g, w)


def _mlp_kernel(ff_slab, final_norm, x_ref, oa_ref, ob_ref, wout_ref, gm_ref, wup_ref, wdn_ref, gf_ref, y_ref):
    mix = jnp.concatenate([oa_ref[...], ob_ref[...]], axis=1)
    x1 = x_ref[...] + jnp.dot(mix, wout_ref[...], preferred_element_type=F32)
    hm = _rms(x1, gm_ref[...]).astype(BF16)
    acc = x1
    d_ff = wup_ref.shape[1]
    for s in range(0, d_ff, ff_slab):
        up = jnp.dot(hm, wup_ref[:, s:s + ff_slab], preferred_element_type=F32)
        act = jnp.square(jnp.maximum(up, 0.0)).astype(BF16)
        acc = acc + jnp.dot(act, wdn_ref[s:s + ff_slab, :], preferred_element_type=F32)
    y_ref[...] = _rms(acc, gf_ref[...]) if final_norm else acc


def _mlp_layer(x2d, oa, ob, w_out, g_mlp, w_up, w_dn, g_final, final_norm, tm):
    n, d = x2d.shape
    row = lambda w: pl.BlockSpec((tm, w), lambda i: (i, 0))
    return pl.pallas_call(
        functools.partial(_mlp_kernel, 1024, final_norm),
        grid=(n // tm,),
        in_specs=[row(d), row(ATT_W), row(GDN_V), _const_spec(w_out.shape), _const_spec((1, d)),
                  _const_spec(w_up.shape), _const_spec(w_dn.shape), _const_spec((1, d))],
        out_specs=row(d),
        out_shape=jax.ShapeDtypeStruct((n, d), F32),
        compiler_params=_params(1),
        name="outproj_mlp",
    )(x2d, oa, ob, w_out, g_mlp, w_up, w_dn, g_final)


def _gdn_kernel(t_valid, bb, qkv_ref, conv0_ref, gate_ref, z_ref, s0_ref, cw_ref, alog_ref, dtb_ref, gn_ref,
                ob_ref, sout_ref, ext_ref, s_ref):
    c = pl.program_id(1)
    n_c = pl.num_programs(1)
    C = CHUNK
    HIST = SUBLANES

    @pl.when(c == 0)
    def _():
        for b in range(bb):
            ext_ref[b, 0:HIST, :] = jnp.zeros((HIST, CONV_CH), F32)
            ext_ref[b, HIST - (CONV_W - 1):HIST, :] = conv0_ref[b]
        s_ref[...] = s0_ref[...]

    H = N_HEADS_GDN
    R = H * C
    rows = lax.broadcasted_iota(I32, (R, R), 0)
    cols = lax.broadcasted_iota(I32, (R, R), 1)
    same_head = (rows // C) == (cols // C)
    incl = jnp.logical_and(same_head, rows >= cols)
    strict = jnp.logical_and(same_head, rows > cols)
    l_incl = (lax.broadcasted_iota(I32, (C, C), 0) >= lax.broadcasted_iota(I32, (C, C), 1)).astype(F32)
    gn = gn_ref[...]
    n_double = int(np.log2(C)) - 1
    stack = lambda f: jnp.concatenate([f(h) for h in range(H)], axis=0)
    head = lambda x, h: x[h * C:(h + 1) * C]

    for b in range(bb):
        ext_ref[b, HIST:HIST + C, :] = qkv_ref[b]
        conv = ext_ref[b, HIST - 3:HIST - 3 + C, :] * cw_ref[0:1, :]
        for j in range(1, CONV_W):
            conv = conv + ext_ref[b, HIST - 3 + j:HIST - 3 + j + C, :] * cw_ref[j:j + 1, :]
        conv = conv * _sigmoid(conv)
        ext_ref[b, 0:HIST, :] = ext_ref[b, C:C + HIST, :]

        gates = gate_ref[b]
        g_slab = -jnp.exp(alog_ref[...]) * _softplus(gates + dtb_ref[...])
        beta_slab = _sigmoid(gates)
        if t_valid < C:
            live = lax.broadcasted_iota(I32, (C, LANES), 0) < t_valid
            g_slab = jnp.where(live, g_slab, 0.0)
            beta_slab = jnp.where(live, beta_slab, 0.0)
        gc_slab = jnp.dot(l_incl, g_slab, precision=HI, preferred_element_type=F32)
        gc_rows = gc_slab.T
        z = z_ref[b]

        q = stack(lambda h: conv[:, h * HEAD_DIM_K:(h + 1) * HEAD_DIM_K])
        k = stack(lambda h: conv[:, GDN_K + h * HEAD_DIM_K:GDN_K + (h + 1) * HEAD_DIM_K])
        v = stack(lambda h: conv[:, 2 * GDN_K + h * HEAD_DIM_V:2 * GDN_K + (h + 1) * HEAD_DIM_V])
        zz = stack(lambda h: z[:, h * HEAD_DIM_V:(h + 1) * HEAD_DIM_V])
        q = q * lax.rsqrt(jnp.sum(q * q, axis=-1, keepdims=True) + EPS) * (HEAD_DIM_K ** -0.5)
        k = k * lax.rsqrt(jnp.sum(k * k, axis=-1, keepdims=True) + EPS)
        gc = stack(lambda h: gc_slab[:, GATE_A + h:GATE_A + h + 1])
        beta = stack(lambda h: beta_slab[:, GATE_B + h:GATE_B + h + 1])
        gc_row = jnp.concatenate([gc_rows[GATE_A + h:GATE_A + h + 1, :] for h in range(H)], axis=1)
        g_last = stack(lambda h: jnp.broadcast_to(gc_slab[C - 1:C, GATE_A + h:GATE_A + h + 1], (C, 1)))
        eg = jnp.exp(gc)
        kb = k * beta
        vb = v * beta
        decay = jnp.exp(jnp.where(incl, gc - gc_row, NEG_INF))
        kk = lax.dot_general(kb, k, NT_DIMS, preferred_element_type=F32)
        qk = lax.dot_general(q, k, NT_DIMS, preferred_element_type=F32)
        x = -jnp.where(strict, kk * decay, 0.0)
        aqk = jnp.where(incl, qk * decay, 0.0)
        r = jnp.concatenate([vb, kb * eg], axis=1)
        xs = _split_bf16(x)
        r = r + _dot_split(xs, _split_bf16(r))
        pw = _dot_split(xs, xs)
        r = r + _dot_split(_split_bf16(pw), _split_bf16(r))
        for _ in range(n_double - 1):
            pw = jnp.dot(pw, pw, preferred_element_type=F32)
            r = r + jnp.dot(pw, r, preferred_element_type=F32)
        u = r[:, :HEAD_DIM_V]
        w = r[:, HEAD_DIM_V:]
        s_old = [s_ref[b, h] for h in range(H)]
        v_new = u - stack(lambda h: jnp.dot(head(w, h), s_old[h], preferred_element_type=F32))
        o = (stack(lambda h: jnp.dot(head(q * eg, h), s_old[h], preferred_element_type=F32))
             + jnp.dot(aqk, v_new, preferred_element_type=F32))
        kd = k * jnp.exp(g_last - gc)
        for h in range(H):
            s_ref[b, h] = (s_old[h] * jnp.exp(g_last[h * C:h * C + 1, :])
                           + lax.dot_general(head(kd, h), head(v_new, h), TN_DIMS, preferred_element_type=F32))
        o = (_rms(o, gn) * (zz * _sigmoid(zz))).astype(ob_ref.dtype)
        for h in range(H):
            ob_ref[b, :, h * HEAD_DIM_V:(h + 1) * HEAD_DIM_V] = head(o, h)

    @pl.when(c == n_c - 1)
    def _():
        sout_ref[...] = s_ref[...]


def _gdn(qkv, conv0, gates, z, s0, conv_w, alog_row, dtb_row, gn, t_valid, bb):
    b, t, _ = qkv.shape
    n_c = t // CHUNK
    blk = lambda w: pl.BlockSpec((bb, CHUNK, w), lambda i, c: (i, c, 0))
    per_b = lambda shape: pl.BlockSpec((bb,) + shape, lambda i, c: (i,) + (0,) * len(shape))
    state = (N_HEADS_GDN, HEAD_DIM_K, HEAD_DIM_V)
    return pl.pallas_call(
        functools.partial(_gdn_kernel, t_valid, bb),
        grid=(b // bb, n_c),
        in_specs=[blk(CONV_CH), per_b((CONV_W - 1, CONV_CH)), blk(LANES), blk(GDN_V), per_b(state),
                  _const_spec((CONV_W, CONV_CH)), _const_spec((1, LANES)), _const_spec((1, LANES)),
                  _const_spec((1, HEAD_DIM_V))],
        out_specs=(blk(GDN_V), per_b(state)),
        out_shape=(jax.ShapeDtypeStruct((b, t, GDN_V), BF16), jax.ShapeDtypeStruct((b,) + state, F32)),
        scratch_shapes=[pltpu.VMEM((bb, CHUNK + 2 * SUBLANES, CONV_CH), F32), pltpu.VMEM((bb,) + state, F32)],
        compiler_params=_params(2),
        name="gdn",
    )(qkv, conv0, gates, z, s0, conv_w, alog_row, dtb_row, gn)


def _topk_threshold(count_ge, qshape, n_sel):
    def bis(it, u):
        cu = u | lax.shift_left(jnp.int32(1), 31 - it)
        return jnp.where(count_ge(_key_to_float(cu ^ INT_MIN)) >= n_sel, cu, u)

    thr_key = lax.fori_loop(0, 32, bis, jnp.zeros(qshape, I32)) ^ INT_MIN
    return thr_key, _key_to_float(thr_key), _key_to_float(thr_key + 1)


def _topk_rule(count_ge, count_tie_le, qshape, n_sel, n_idx_bits, j_scr):
    thr_key, thr, nxt = _topk_threshold(count_ge, qshape, n_sel)
    need = n_sel - count_ge(nxt)
    excess = jnp.logical_and(count_ge(thr) > n_sel, thr_key > KEY_NEG)
    j_scr[...] = jnp.full(qshape, np.int32(2 ** 30), I32)

    @pl.when(jnp.max(excess.astype(I32)) > 0)
    def _():
        def jbis(it, j):
            cj = j | lax.shift_left(jnp.int32(1), n_idx_bits - 1 - it)
            return jnp.where(count_tie_le(thr, nxt, cj - 1) < need, cj, j)

        j_scr[...] = lax.fori_loop(0, n_idx_bits, jbis, jnp.zeros(qshape, I32))

    return thr, nxt, j_scr[...]


def _dsa_prompt_kernel(n_sel, tq, ck, t_total, qi_ref, ki2_ref, gate_ref, q_ref, k_ref, vt_ref,
                       o_ref, sc_scr, s_scr_even, s_scr_odd):
    s_scrs = (s_scr_even, s_scr_odd)
    i = pl.program_id(1)
    n_chunks = lax.div(i * tq + tq + ck - 1, ck)
    grp = ck // SUBLANES
    g3 = (grp, SUBLANES, tq)
    q_id = i * tq + lax.broadcasted_iota(I32, g3, 2)
    k_in_chunk = lax.broadcasted_iota(I32, g3, 0) * SUBLANES + lax.broadcasted_iota(I32, g3, 1)
    lane_lo = lax.broadcasted_iota(I32, (tq, LANES), 1) < HEAD_DIM_ATT
    rep8 = lambda v: jnp.broadcast_to(v, (SUBLANES, tq))

    qi = qi_ref[...] * (IDX_DIM ** -0.5)
    wi_t = gate_ref[...].T[GATE_WI:GATE_WI + N_IDX_HEADS, :] * (N_IDX_HEADS ** -0.5)
    qi_heads = []
    for h in range(N_IDX_HEADS):
        pair = qi[:, (h // 2) * LANES:(h // 2 + 1) * LANES]
        keep = lane_lo if h % 2 == 0 else jnp.logical_not(lane_lo)
        qi_heads.append(jnp.where(keep, pair, jnp.zeros_like(pair)))

    def score_body(c, carry):
        k0 = pl.multiple_of(c * ck, ck)
        kic = ki2_ref[pl.ds(k0, ck), :]
        sc = jnp.zeros((ck, tq), F32)
        for h in range(N_IDX_HEADS):
            logit = lax.dot_general(kic, qi_heads[h], NT_DIMS, preferred_element_type=F32)
            sc = sc + jnp.maximum(logit, 0.0) * wi_t[h:h + 1, :]
        causal = k0 + k_in_chunk <= q_id
        sc_scr[c] = jnp.where(causal, sc.reshape(g3), NEG_INF).reshape(ck, tq)
        return carry

    lax.fori_loop(0, n_chunks, score_body, 0)

    acc_shape = (N_ACC, SUBLANES, tq)
    fold = lambda x, op: op(x.reshape(grp // N_ACC, N_ACC, SUBLANES, tq), axis=0)
    finish = lambda acc, op: op(op(acc, axis=0), axis=0, keepdims=True)

    def count_ge(cand):
        cand8 = rep8(cand)
        body = lambda c, acc: acc + fold(jnp.where(sc_scr[c].reshape(g3) >= cand8, 1.0, 0.0), jnp.sum)
        return finish(lax.fori_loop(0, n_chunks, body, jnp.zeros(acc_shape, F32)), jnp.sum)

    _, thr, nxt = _topk_threshold(count_ge, (1, tq), n_sel)
    need = n_sel - count_ge(nxt)
    tri = (lax.broadcasted_iota(I32, (ck, ck), 0) >= lax.broadcasted_iota(I32, (ck, ck), 1)).astype(BF16)

    def mask_body(c, seen):
        s = sc_scr[c]
        above = s >= nxt
        tie = jnp.logical_and(s >= thr, jnp.logical_not(above))
        rank = seen + jnp.dot(tri, jnp.where(tie, 1.0, 0.0).astype(BF16), preferred_element_type=F32)
        sel = jnp.logical_or(above, jnp.logical_and(tie, rank <= need))
        sel = jnp.logical_and(sel, s > NEG_INF)
        sc_scr[c] = jnp.where(sel, 0.0, NEG_INF)
        return rank[ck - 1:ck, :]

    lax.fori_loop(0, n_chunks, mask_body, jnp.zeros((1, tq), F32))

    def q_masked(h):
        qp = q_ref[:, (h // 2) * LANES:(h // 2 + 1) * LANES] * (HEAD_DIM_ATT ** -0.5)
        keep = lane_lo if h % 2 == 0 else jnp.logical_not(lane_lo)
        return jnp.where(keep, qp, jnp.zeros_like(qp))

    sub = LANES
    fold_sub = lambda x, op: op(x.reshape(sub // (N_ACC * SUBLANES), N_ACC, SUBLANES, tq), axis=0)

    def s_pass(h, qm, c, macc):
        k0 = pl.multiple_of(c * ck, ck)
        for r in range(0, ck, sub):
            kc = k_ref[pl.ds(k0 + r, sub), (h // 2) * LANES:(h // 2 + 1) * LANES]
            st = lax.dot_general(kc, qm, NT_DIMS, preferred_element_type=F32) + sc_scr[c, r:r + sub, :]
            s_scrs[h % 2][c, r:r + sub, :] = st
            macc = jnp.maximum(macc, fold_sub(st, jnp.max))
        return macc

    def pv_pass(h, m, c, lacc, oacc):
        for r in range(0, ck, sub):
            pt = jnp.exp(s_scrs[h % 2][c, r:r + sub, :] - m)
            vt = vt_ref[c, h * HEAD_DIM_ATT:(h + 1) * HEAD_DIM_ATT, r:r + sub]
            lacc = lacc + fold_sub(pt, jnp.sum)
            oacc = oacc + jnp.dot(vt, pt.astype(BF16), preferred_element_type=F32)
        return lacc, oacc

    neg = jnp.full(acc_shape, NEG_INF, F32)
    zero_l = jnp.zeros(acc_shape, F32)
    zero_o = jnp.zeros((HEAD_DIM_ATT, tq), F32)
    qm0 = q_masked(0)
    macc = lax.fori_loop(0, n_chunks, lambda c, a: s_pass(0, qm0, c, a), neg)
    o_rows = []
    for h in range(N_HEADS_ATT):
        m = finish(macc, jnp.max)
        if h + 1 < N_HEADS_ATT:
            qm_next = q_masked(h + 1)

            def body(c, carry, h=h, m=m, qm_next=qm_next):
                lacc, oacc, macc_next = carry
                macc_next = s_pass(h + 1, qm_next, c, macc_next)
                lacc, oacc = pv_pass(h, m, c, lacc, oacc)
                return lacc, oacc, macc_next

            lacc, oacc, macc = lax.fori_loop(0, n_chunks, body, (zero_l, zero_o, neg))
        else:
            lacc, oacc = lax.fori_loop(0, n_chunks, lambda c, cr, h=h, m=m: pv_pass(h, m, c, *cr), (zero_l, zero_o))
        o_rows.append(oacc / finish(lacc, jnp.sum))
    o_ref[...] = jnp.concatenate(o_rows, axis=0).T.astype(o_ref.dtype)


def _dsa_prompt(qib, ki2b, gates, qab, kab, vtb, batch, t, n_sel, tq, ck):
    n_q = t // tq
    tile = lambda w: pl.BlockSpec((tq, w), lambda b, i: (b * n_q + i, 0))
    seq = lambda w: pl.BlockSpec((t, w), lambda b, i: (b, 0))
    return pl.pallas_call(
        functools.partial(_dsa_prompt_kernel, n_sel, tq, ck, t),
        grid=(batch, n_q),
        in_specs=[tile(IDX_W), seq(2 * IDX_DIM), tile(LANES), tile(ATT_W), seq(ATT_W),
                  pl.BlockSpec((None, t // ck, ATT_W, ck), lambda b, i: (b, 0, 0, 0))],
        out_specs=tile(ATT_W),
        out_shape=jax.ShapeDtypeStruct((batch * t, ATT_W), BF16),
        scratch_shapes=[pltpu.VMEM((t // ck, ck, tq), F32), pltpu.VMEM((t // ck, ck, tq), F32),
                        pltpu.VMEM((t // ck, ck, tq), F32)],
        compiler_params=_params(2),
        name="dsa_prompt",
    )(qib, ki2b, gates, qab, kab, vtb)


def _dsa_sample_kernel(n_sel, n_pages, pg, pt_ref, qi_ref, wcol_ref, kinew_ref, q_ref, knew_ref, vnew_ref, *refs):
    ki_refs, k_refs, v_refs = refs[:pg], refs[pg:2 * pg], refs[2 * pg:3 * pg]
    o_ref = refs[3 * pg]
    sc_scr, s_scr, macc_scr, lacc_scr, oacc_scr, j_scr = refs[3 * pg + 1:]
    del pt_ref
    s = pl.program_id(1)
    nst = n_pages // pg
    n_ch = n_pages + 1
    tq = SUBLANES
    page = LANES
    qshape = (tq, page)
    lane = lax.broadcasted_iota(I32, qshape, 1)

    def new_t(x):
        wdt = x.shape[1]
        wp = max(wdt, page)
        if wp > wdt:
            x = jnp.concatenate([x, jnp.zeros((tq, wp - wdt), F32)], axis=1)
        sq = jnp.concatenate([x, jnp.zeros((page - tq, wp), F32)], axis=0)
        return jnp.concatenate([sq[:, t * page:(t + 1) * page].T for t in range(wp // page)], axis=0)[:wdt]

    @pl.when(s < nst)
    def _():
        qi = qi_ref[...] * (IDX_DIM ** -0.5)
        wcol = wcol_ref[...] * (N_IDX_HEADS ** -0.5)

        def score_chunk(ki_t):
            logit = jnp.dot(qi, ki_t.astype(BF16), preferred_element_type=F32)
            lw = jnp.maximum(logit, 0.0) * wcol
            sc = lw[0:tq]
            for h in range(1, N_IDX_HEADS):
                sc = sc + lw[h * tq:(h + 1) * tq]
            return sc

        for p in range(pg):
            sc_scr[s * pg + p] = score_chunk(ki_refs[p][...])

        @pl.when(s == 0)
        def _():
            sc = score_chunk(new_t(kinew_ref[...]))
            causal = lane <= lax.broadcasted_iota(I32, qshape, 0)
            sc_scr[n_pages] = jnp.where(causal, sc, NEG_INF)

        @pl.when(s == nst - 1)
        def _():
            def row_total(hits):
                accs = [jnp.zeros(qshape, F32)] * N_ACC
                for c in range(n_ch):
                    accs[c % N_ACC] = accs[c % N_ACC] + jnp.where(hits(c), 1.0, 0.0)
                tot = functools.reduce(lambda a, b: a + b, accs)
                return jnp.broadcast_to(jnp.sum(tot, axis=1, keepdims=True), qshape)

            def count_ge(cand):
                return row_total(lambda c: sc_scr[c] >= cand)

            def tie(sc, thr, nxt, c, j):
                return jnp.logical_and(jnp.logical_and(sc >= thr, jnp.logical_not(sc >= nxt)), c * page + lane <= j)

            def count_tie_le(thr, nxt, j):
                return row_total(lambda c: tie(sc_scr[c], thr, nxt, c, j))

            thr, nxt, jmax = _topk_rule(count_ge, count_tie_le, qshape, n_sel,
                                        int(np.ceil(np.log2(n_ch * page))), j_scr)

            def bias_body(c, carry):
                sc = sc_scr[c]
                sel = jnp.logical_or(sc >= nxt, tie(sc, thr, nxt, c, jmax))
                sel = jnp.logical_and(sel, sc > NEG_INF)
                sc_scr[c] = jnp.where(sel, 0.0, NEG_INF)
                return carry

            lax.fori_loop(0, n_ch, bias_body, 0)

    @pl.when(jnp.logical_and(s >= nst, s < 2 * nst))
    def _():
        j = s - nst
        q = q_ref[...] * (HEAD_DIM_ATT ** -0.5)

        def s_chunk(kt_of_head, c):
            bias = sc_scr[c]
            parts = []
            for h in range(N_HEADS_ATT):
                qh = q[:, h * HEAD_DIM_ATT:(h + 1) * HEAD_DIM_ATT]
                parts.append(jnp.dot(qh, kt_of_head(h).astype(BF16), preferred_element_type=F32) + bias)
            sc = jnp.concatenate(parts, axis=0)
            s_scr[c] = sc
            return sc

        @pl.when(j == 0)
        def _():
            knew_t = new_t(knew_ref[...])
            macc_scr[...] = s_chunk(lambda h: knew_t[h * HEAD_DIM_ATT:(h + 1) * HEAD_DIM_ATT], n_pages)

        macc = macc_scr[...]
        for p in range(pg):
            macc = jnp.maximum(macc, s_chunk(lambda h, p=p: k_refs[p][h], j * pg + p))
        macc_scr[...] = macc

    @pl.when(s >= 2 * nst)
    def _():
        j = s - 2 * nst
        m = jnp.max(macc_scr[...], axis=1, keepdims=True)

        def pv_chunk(vt_of_head, c):
            pch = jnp.exp(s_scr[c] - m)
            pb = pch.astype(BF16)
            outs = [lax.dot_general(pb[h * tq:(h + 1) * tq], vt_of_head(h).astype(BF16), NT_DIMS,
                                    preferred_element_type=F32) for h in range(N_HEADS_ATT)]
            return pch, jnp.concatenate(outs, axis=0)

        @pl.when(j == 0)
        def _():
            vnew_t = new_t(vnew_ref[...])
            l0, o0 = pv_chunk(lambda h: vnew_t[h * HEAD_DIM_ATT:(h + 1) * HEAD_DIM_ATT], n_pages)
            lacc_scr[...] = l0
            oacc_scr[...] = o0

        lacc = lacc_scr[...]
        oacc = oacc_scr[...]
        for p in range(pg):
            lp, op = pv_chunk(lambda h, p=p: v_refs[p][h], j * pg + p)
            lacc = lacc + lp
            oacc = oacc + op
        lacc_scr[...] = lacc
        oacc_scr[...] = oacc

        @pl.when(j == nst - 1)
        def _():
            o = oacc / jnp.sum(lacc, axis=1, keepdims=True)
            o_ref[...] = jnp.concatenate([o[h * tq:(h + 1) * tq] for h in range(N_HEADS_ATT)], axis=1)


def _dsa_sample(layer, page_table, qi, wcol, ki_new, q, k_new, v_new, cache_ki_t, cache_k_t, cache_v_t, n_sel, pg):
    b, n_pages = page_table.shape
    page = cache_k_t.shape[-1]
    assert page == LANES and n_pages % pg == 0 and ki_new.shape[1] == SUBLANES
    nst = n_pages // pg
    per_b = lambda shape: pl.BlockSpec((None,) + shape, lambda i, s, pt: (i,) + (0,) * len(shape))

    def page_spec(lead, phase, p):
        def idx(i, s, pt):
            step = jnp.clip(s - phase * nst, 0, nst - 1)
            return (layer, pt[i, step * pg + p]) + (0,) * (len(lead) + 1)
        return pl.BlockSpec((None, None) + lead + (page,), idx)

    hq_idx = N_IDX_HEADS * SUBLANES
    in_specs = [per_b((hq_idx, IDX_DIM)), per_b((hq_idx, 1)), per_b((SUBLANES, IDX_DIM)), per_b((SUBLANES, ATT_W)),
                per_b((SUBLANES, ATT_W)), per_b((SUBLANES, ATT_W))]
    in_specs += [page_spec((IDX_DIM,), 0, p) for p in range(pg)]
    in_specs += [page_spec((N_HEADS_ATT, HEAD_DIM_ATT), 1, p) for p in range(pg)]
    in_specs += [page_spec((N_HEADS_ATT, HEAD_DIM_ATT), 2, p) for p in range(pg)]
    hq = N_HEADS_ATT * SUBLANES
    grid_spec = pltpu.PrefetchScalarGridSpec(
        num_scalar_prefetch=1,
        grid=(b, 3 * nst),
        in_specs=in_specs,
        out_specs=per_b((SUBLANES, ATT_W)),
        scratch_shapes=[pltpu.VMEM((n_pages + 1, SUBLANES, LANES), F32), pltpu.VMEM((n_pages + 1, hq, LANES), F32),
                        pltpu.VMEM((hq, LANES), F32), pltpu.VMEM((hq, LANES), F32),
                        pltpu.VMEM((hq, HEAD_DIM_ATT), F32), pltpu.VMEM((SUBLANES, LANES), I32)],
    )
    return pl.pallas_call(
        functools.partial(_dsa_sample_kernel, n_sel, n_pages, pg),
        grid_spec=grid_spec,
        out_shape=jax.ShapeDtypeStruct((b, SUBLANES, ATT_W), F32),
        compiler_params=_params(2),
        name="dsa_sample",
    )(page_table, qi, wcol, ki_new, q, k_new, v_new,
      *([cache_ki_t] * pg), *([cache_k_t] * pg), *([cache_v_t] * pg))


def _regroup_w_in(w):
    o_qi = 3 * ATT_W
    o_ki = o_qi + IDX_W
    o_wi = o_ki + IDX_DIM
    o_cv = o_wi + N_IDX_HEADS
    o_a = o_cv + CONV_CH
    o_b = o_a + N_HEADS_GDN
    o_z = o_b + N_HEADS_GDN
    gate_cols = jnp.concatenate([w[:, o_wi:o_cv], w[:, o_a:o_b], w[:, o_b:o_z]], axis=1)
    gate_cols = jnp.pad(gate_cols, ((0, 0), (0, LANES - gate_cols.shape[1])))
    out = jnp.concatenate([w[:, :o_qi], w[:, o_cv:o_a], w[:, o_z:o_z + GDN_V], w[:, o_qi:o_ki],
                           w[:, o_ki:o_wi], w[:, o_ki:o_wi], gate_cols], axis=1)
    assert out.shape[1] == PJ_W
    return out.astype(BF16)


def _gate_row(v):
    return jnp.pad(v.astype(F32), (GATE_A, LANES - GATE_A - v.shape[0]))[None, :]


def _tiles(bp, tp, bs, ts, n_pages):
    ck = min(512, tp)
    return dict(
        tm_p=ck,
        tm_s=min(256, bs * ts),
        tq=min(256, tp),
        ck=ck,
        bb_p=min(4, bp),
        bb_s=min(4, bs),
        pg=min(16, n_pages),
    )


def kernel(x_prompt, x_sample, cache_k, cache_v, cache_kidx, state_ssm, state_conv, page_table, norm_mix, w_in,
           conv_w, a_log, dt_bias, norm_gdn_out, w_out, norm_mlp, w_up, w_down, norm_final):
    depth = w_in.shape[0]
    bp, tp, d = x_prompt.shape
    bs, ts, _ = x_sample.shape
    page = cache_k.shape[2]
    past = page_table.shape[1] * page
    cfg = _tiles(bp, tp, bs, ts, page_table.shape[1])
    n_sel_p = min(TOP_K_MAX, tp // 4)
    n_sel_s = min(TOP_K_MAX, (past + ts) // 4)
    assert tp % CHUNK == 0 and CONV_W - 1 <= ts <= CHUNK
    assert cfg["tq"] >= n_sel_p and tp % cfg["tq"] == 0 and tp % cfg["ck"] == 0 and cfg["ck"] % cfg["tq"] == 0
    assert bp % cfg["bb_p"] == 0 and bs % cfg["bb_s"] == 0

    cache_k_t = jnp.transpose(cache_k, (0, 1, 3, 4, 2))
    cache_v_t = jnp.transpose(cache_v, (0, 1, 3, 4, 2))
    cache_ki_t = jnp.transpose(cache_kidx, (0, 1, 3, 2))

    yp = x_prompt.reshape(bp * tp, d)
    ys = x_sample.reshape(bs * ts, d)
    outs = {k: [] for k in ("kp", "vp", "kip", "sp", "cp", "ks", "vs", "kis", "ss", "cs")}
    for l in range(depth):
        w_pj = _regroup_w_in(w_in[l])
        g_mix = norm_mix[l][None, :]
        alog_row, dtb_row = _gate_row(a_log[l]), _gate_row(dt_bias[l])
        gn = norm_gdn_out[l][None, :]
        w_out_b, w_up_b, w_dn_b = w_out[l].astype(BF16), w_up[l].astype(BF16), w_down[l].astype(BF16)
        g_mlp = norm_mlp[l][None, :]
        g_fin = norm_final[None, :]
        last = l == depth - 1

        qab, kab, vtb, kat, vat, kit, qkv, z, qib, ki2b, gates = _inproj(yp, g_mix, w_pj, cfg["tm_p"], tp, True)
        oa = _dsa_prompt(qib, ki2b, gates, qab, kab, vtb, bp, tp, n_sel_p, cfg["tq"], cfg["ck"])
        ob, s_new = _gdn(qkv.reshape(bp, tp, CONV_CH), jnp.zeros((bp, CONV_W - 1, CONV_CH), F32),
                         gates.reshape(bp, tp, LANES), z.reshape(bp, tp, GDN_V),
                         jnp.zeros((bp, N_HEADS_GDN, HEAD_DIM_K, HEAD_DIM_V), F32),
                         conv_w[l], alog_row, dtb_row, gn, CHUNK, cfg["bb_p"])
        outs["kp"].append(jnp.transpose(kat, (0, 3, 1, 2)))
        outs["vp"].append(jnp.transpose(vat, (0, 3, 1, 2)))
        outs["kip"].append(jnp.transpose(kit, (0, 2, 1)))
        outs["sp"].append(s_new.astype(state_ssm.dtype))
        outs["cp"].append(qkv.reshape(bp, tp, CONV_CH)[:, tp - (CONV_W - 1):, :])
        yp = _mlp_layer(yp, oa, ob.reshape(bp * tp, GDN_V), w_out_b, g_mlp, w_up_b, w_dn_b, g_fin, last, cfg["tm_p"])

        qab, ka, va, ki, qkv, z, qib, gates = _inproj(ys, g_mix, w_pj, cfg["tm_s"], ts, False)
        qi_rows = qib.reshape(bs, ts, N_IDX_HEADS, IDX_DIM).transpose(0, 2, 1, 3).reshape(bs, N_IDX_HEADS * ts, IDX_DIM)
        wcol = gates[:, GATE_WI:GATE_WI + N_IDX_HEADS].reshape(bs, ts, N_IDX_HEADS)
        wcol = wcol.transpose(0, 2, 1).reshape(bs, N_IDX_HEADS * ts, 1)
        oa = _dsa_sample(l, page_table, qi_rows, wcol, ki.reshape(bs, ts, IDX_DIM), qab.reshape(bs, ts, ATT_W),
                         ka.reshape(bs, ts, ATT_W), va.reshape(bs, ts, ATT_W),
                         cache_ki_t, cache_k_t, cache_v_t, n_sel_s, cfg["pg"])
        pad_t = lambda a: jnp.pad(a.reshape(bs, ts, a.shape[-1]), ((0, 0), (0, CHUNK - ts), (0, 0)))
        ob, s_new = _gdn(pad_t(qkv), state_conv[l], pad_t(gates), pad_t(z), state_ssm[l].astype(F32),
                         conv_w[l], alog_row, dtb_row, gn, ts, cfg["bb_s"])
        outs["ks"].append(ka.reshape(bs, ts, N_HEADS_ATT, HEAD_DIM_ATT))
        outs["vs"].append(va.reshape(bs, ts, N_HEADS_ATT, HEAD_DIM_ATT))
        outs["kis"].append(ki.reshape(bs, ts, IDX_DIM))
        outs["ss"].append(s_new.astype(state_ssm.dtype))
        outs["cs"].append(qkv.reshape(bs, ts, CONV_CH)[:, ts - (CONV_W - 1):, :])
        ys = _mlp_layer(ys, oa.reshape(bs * ts, ATT_W).astype(BF16), ob[:, :ts, :].reshape(bs * ts, GDN_V),
                        w_out_b, g_mlp, w_up_b, w_dn_b, g_fin, last, cfg["tm_s"])

    st = lambda k: jnp.stack(outs[k])
    return (yp.reshape(bp, tp, d), ys.reshape(bs, ts, d),
            st("kp"), st("vp"), st("kip"), st("sp"), st("cp"),
            st("ks"), st("vs"), st("kis"), st("ss"), st("cs"))
```

```python
import functools

import numpy as np
import jax
import jax.numpy as jnp
from jax import lax
from jax.experimental import pallas as pl
from jax.experimental.pallas import tpu as pltpu

F32 = jnp.float32
BF16 = jnp.bfloat16
I32 = jnp.int32
HI = lax.Precision.HIGHEST

EPS = 1e-6
N_HEADS_ATT = 8
HEAD_DIM_ATT = 64
ATT_W = N_HEADS_ATT * HEAD_DIM_ATT
N_IDX_HEADS = 4
IDX_DIM = 64
IDX_W = N_IDX_HEADS * IDX_DIM
TOP_K_MAX = 256
N_HEADS_GDN = 4
HEAD_DIM_K = 128
HEAD_DIM_V = 128
GDN_K = N_HEADS_GDN * HEAD_DIM_K
GDN_V = N_HEADS_GDN * HEAD_DIM_V
CONV_W = 4
CONV_CH = 2 * GDN_K + GDN_V
CHUNK = 64

LANES = 128
SUBLANES = 8
VMEM_LIMIT = 56 * 1024 * 1024
N_ACC = 4

PJ_ATT = 0
PJ_CV = PJ_ATT + 3 * ATT_W
PJ_Z = PJ_CV + CONV_CH
PJ_QI = PJ_Z + GDN_V
PJ_KI = PJ_QI + IDX_W
PJ_GATE = PJ_KI + 2 * IDX_DIM
PJ_W = PJ_GATE + LANES
GATE_WI = 0
GATE_A = 4
GATE_B = 8

INT_MIN = np.int32(-2 ** 31)
KEY_NEG = np.int32(np.uint32(0x807FFFFF).view(np.int32))
NEG_INF = float("-inf")

NT_DIMS = (((1,), (1,)), ((), ()))
TN_DIMS = (((0,), (0,)), ((), ()))


def _key_to_float(key):
    bits = jnp.where(key < 0, key ^ np.int32(0x7FFFFFFF), key)
    return jnp.where(key < KEY_NEG, NEG_INF, lax.bitcast_convert_type(bits, F32))


def _sigmoid(x):
    return 1.0 / (1.0 + jnp.exp(-x))


def _softplus(x):
    return jnp.maximum(x, 0.0) + jnp.log1p(jnp.exp(-jnp.abs(x)))


def _split_bf16(x):
    hi = x.astype(BF16)
    return hi, (x - hi.astype(F32)).astype(BF16)


def _dot_split(a, b):
    dot = lambda p, q: jnp.dot(p, q, preferred_element_type=F32)
    return dot(a[0], b[0]) + (dot(a[0], b[1]) + dot(a[1], b[0]))


def _rms(x, g):
    return x * lax.rsqrt(jnp.mean(x * x, axis=-1, keepdims=True) + EPS) * g


def _const_spec(shape):
    return pl.BlockSpec(shape, lambda *_: (0,) * len(shape), pipeline_mode=pl.Buffered(1))


def _params(n_grid_dims):
    return pltpu.CompilerParams(dimension_semantics=("arbitrary",) * n_grid_dims, vmem_limit_bytes=VMEM_LIMIT)


def _inproj_kernel(cache_t, x_ref, g_ref, w_ref, *out_refs):
    hb = _rms(x_ref[...], g_ref[...]).astype(BF16)
    proj = jnp.dot(hb, w_ref[...], preferred_element_type=F32)
    qa = proj[:, PJ_ATT:PJ_ATT + ATT_W]
    ka = proj[:, PJ_ATT + ATT_W:PJ_ATT + 2 * ATT_W]
    va = proj[:, PJ_ATT + 2 * ATT_W:PJ_ATT + 3 * ATT_W]
    qkv = proj[:, PJ_CV:PJ_CV + CONV_CH]
    z = proj[:, PJ_Z:PJ_Z + GDN_V]
    qi = proj[:, PJ_QI:PJ_QI + IDX_W]
    ki2 = proj[:, PJ_KI:PJ_KI + 2 * IDX_DIM]
    gates = proj[:, PJ_GATE:PJ_GATE + LANES]
    if cache_t:
        (qab_ref, kab_ref, vtb_ref, kat_ref, vat_ref, kit_ref, qkv_ref, z_ref, qib_ref, ki2b_ref, gate_ref) = out_refs
        tm = ka.shape[0]
        kab_ref[...] = ka.astype(BF16)
        ki2b_ref[...] = ki2.astype(BF16)
        vat = va.T
        vtb_ref[...] = vat.astype(BF16)
        kat_ref[...] = ka.T.reshape(N_HEADS_ATT, HEAD_DIM_ATT, tm)
        vat_ref[...] = vat.reshape(N_HEADS_ATT, HEAD_DIM_ATT, tm)
        kit_ref[...] = ki2.T[:IDX_DIM, :]
    else:
        (qab_ref, ka_ref, va_ref, ki_ref, qkv_ref, z_ref, qib_ref, gate_ref) = out_refs
        ka_ref[...] = ka
        va_ref[...] = va
        ki_ref[...] = ki2[:, :IDX_DIM]
    qab_ref[...] = qa.astype(BF16)
    qkv_ref[...] = qkv
    z_ref[...] = z
    qib_ref[...] = qi.astype(BF16)
    gate_ref[...] = gates


def _inproj(x2d, g, w, tm, seq_len, cache_t):
    n, d = x2d.shape
    row = lambda w_: pl.BlockSpec((tm, w_), lambda i: (i, 0))
    sds = jax.ShapeDtypeStruct
    common = [(row(CONV_CH), sds((n, CONV_CH), F32)), (row(GDN_V), sds((n, GDN_V), F32)),
              (row(IDX_W), sds((n, IDX_W), BF16)), (row(LANES), sds((n, LANES), F32))]
    if cache_t:
        assert seq_len % tm == 0
        n_t = seq_len // tm
        b = n // seq_len
        tok = lambda *lead: pl.BlockSpec((None,) + lead + (tm,), lambda i: (i // n_t,) + (0,) * len(lead) + (i % n_t,))
        head_t = (N_HEADS_ATT, HEAD_DIM_ATT)
        out = [(row(ATT_W), sds((n, ATT_W), BF16)),
               (row(ATT_W), sds((n, ATT_W), BF16)),
               (pl.BlockSpec((None, None, ATT_W, tm), lambda i: (i // n_t, i % n_t, 0, 0)),
                sds((b, n_t, ATT_W, tm), BF16)),
               (tok(*head_t), sds((b,) + head_t + (seq_len,), F32)),
               (tok(*head_t), sds((b,) + head_t + (seq_len,), F32)),
               (tok(IDX_DIM), sds((b, IDX_DIM, seq_len), F32)),
               common[0], common[1], common[2],
               (row(2 * IDX_DIM), sds((n, 2 * IDX_DIM), BF16)),
               common[3]]
    else:
        out = [(row(ATT_W), sds((n, ATT_W), BF16)), (row(ATT_W), sds((n, ATT_W), F32)),
               (row(ATT_W), sds((n, ATT_W), F32)), (row(IDX_DIM), sds((n, IDX_DIM), F32))] + common
    return pl.pallas_call(
        functools.partial(_inproj_kernel, cache_t),
        grid=(n // tm,),
        in_specs=[row(d), _const_spec((1, d)), _const_spec(w.shape)],
        out_specs=tuple(o[0] for o in out),
        out_shape=tuple(o[1] for o in out),
        compiler_params=_params(1),
        name="inproj",
    )(x2d, g, w)


def _mlp_kernel(ff_slab, final_norm, x_ref, oa_ref, ob_ref, wout_ref, gm_ref, wup_ref, wdn_ref, gf_ref, y_ref):
    mix = jnp.concatenate([oa_ref[...], ob_ref[...]], axis=1)
    x1 = x_ref[...] + jnp.dot(mix, wout_ref[...], preferred_element_type=F32)
    hm = _rms(x1, gm_ref[...]).astype(BF16)
    acc = x1
    d_ff = wup_ref.shape[1]
    for s in range(0, d_ff, ff_slab):
        up = jnp.dot(hm, wup_ref[:, s:s + ff_slab], preferred_element_type=F32)
        act = jnp.square(jnp.maximum(up, 0.0)).astype(BF16)
        acc = acc + jnp.dot(act, wdn_ref[s:s + ff_slab, :], preferred_element_type=F32)
    y_ref[...] = _rms(acc, gf_ref[...]) if final_norm else acc


def _mlp_layer(x2d, oa, ob, w_out, g_mlp, w_up, w_dn, g_final, final_norm, tm):
    n, d = x2d.shape
    row = lambda w: pl.BlockSpec((tm, w), lambda i: (i, 0))
    return pl.pallas_call(
        functools.partial(_mlp_kernel, 1024, final_norm),
        grid=(n // tm,),
        in_specs=[row(d), row(ATT_W), row(GDN_V), _const_spec(w_out.shape), _const_spec((1, d)),
                  _const_spec(w_up.shape), _const_spec(w_dn.shape), _const_spec((1, d))],
        out_specs=row(d),
        out_shape=jax.ShapeDtypeStruct((n, d), F32),
        compiler_params=_params(1),
        name="outproj_mlp",
    )(x2d, oa, ob, w_out, g_mlp, w_up, w_dn, g_final)


def _gdn_kernel(t_valid, bb, qkv_ref, conv0_ref, gate_ref, z_ref, s0_ref, cw_ref, alog_ref, dtb_ref, gn_ref,
                ob_ref, sout_ref, ext_ref, s_ref):
    c = pl.program_id(1)
    n_c = pl.num_programs(1)
    C = CHUNK
    HIST = SUBLANES

    @pl.when(c == 0)
    def _():
        for b in range(bb):
            ext_ref[b, 0:HIST, :] = jnp.zeros((HIST, CONV_CH), F32)
            ext_ref[b, HIST - (CONV_W - 1):HIST, :] = conv0_ref[b]
        s_ref[...] = s0_ref[...]

    H = N_HEADS_GDN
    R = H * C
    rows = lax.broadcasted_iota(I32, (R, R), 0)
    cols = lax.broadcasted_iota(I32, (R, R), 1)
    same_head = (rows // C) == (cols // C)
    incl = jnp.logical_and(same_head, rows >= cols)
    strict = jnp.logical_and(same_head, rows > cols)
    l_incl = (lax.broadcasted_iota(I32, (C, C), 0) >= lax.broadcasted_iota(I32, (C, C), 1)).astype(F32)
    gn = gn_ref[...]
    n_double = int(np.log2(C)) - 1
    stack = lambda f: jnp.concatenate([f(h) for h in range(H)], axis=0)
    head = lambda x, h: x[h * C:(h + 1) * C]

    convs, s_olds = [], []
    for b in range(bb):
        ext_ref[b, HIST:HIST + C, :] = qkv_ref[b]
        conv = ext_ref[b, HIST - 3:HIST - 3 + C, :] * cw_ref[0:1, :]
        for j in range(1, CONV_W):
            conv = conv + ext_ref[b, HIST - 3 + j:HIST - 3 + j + C, :] * cw_ref[j:j + 1, :]
        convs.append(conv * _sigmoid(conv))
        ext_ref[b, 0:HIST, :] = ext_ref[b, C:C + HIST, :]
        s_olds.append([s_ref[b, h] for h in range(H)])

    prep = []
    for b in range(bb):
        conv = convs[b]
        gates = gate_ref[b]
        g_slab = -jnp.exp(alog_ref[...]) * _softplus(gates + dtb_ref[...])
        beta_slab = _sigmoid(gates)
        if t_valid < C:
            live = lax.broadcasted_iota(I32, (C, LANES), 0) < t_valid
            g_slab = jnp.where(live, g_slab, 0.0)
            beta_slab = jnp.where(live, beta_slab, 0.0)
        gc_slab = jnp.dot(l_incl, g_slab, precision=HI, preferred_element_type=F32)
        gc_rows = gc_slab.T
        z = z_ref[b]

        q = stack(lambda h: conv[:, h * HEAD_DIM_K:(h + 1) * HEAD_DIM_K])
        k = stack(lambda h: conv[:, GDN_K + h * HEAD_DIM_K:GDN_K + (h + 1) * HEAD_DIM_K])
        v = stack(lambda h: conv[:, 2 * GDN_K + h * HEAD_DIM_V:2 * GDN_K + (h + 1) * HEAD_DIM_V])
        zz = stack(lambda h: z[:, h * HEAD_DIM_V:(h + 1) * HEAD_DIM_V])
        q = q * lax.rsqrt(jnp.sum(q * q, axis=-1, keepdims=True) + EPS) * (HEAD_DIM_K ** -0.5)
        k = k * lax.rsqrt(jnp.sum(k * k, axis=-1, keepdims=True) + EPS)
        gc = stack(lambda h: gc_slab[:, GATE_A + h:GATE_A + h + 1])
        beta = stack(lambda h: beta_slab[:, GATE_B + h:GATE_B + h + 1])
        gc_row = jnp.concatenate([gc_rows[GATE_A + h:GATE_A + h + 1, :] for h in range(H)], axis=1)
        g_last = stack(lambda h: jnp.broadcast_to(gc_slab[C - 1:C, GATE_A + h:GATE_A + h + 1], (C, 1)))
        eg = jnp.exp(gc)
        kb = k * beta
        vb = v * beta
        prep.append(dict(q=q, k=k, kb=kb, gdiff=gc - gc_row, r=jnp.concatenate([vb, kb * eg], axis=1), qe=q * eg,
                         kd=k * jnp.exp(g_last - gc), g_last=g_last, zz=zz))

    each = range(bb)
    kk = [lax.dot_general(prep[b]["kb"], prep[b]["k"], NT_DIMS, preferred_element_type=F32) for b in each]
    qk = [lax.dot_general(prep[b]["q"], prep[b]["k"], NT_DIMS, preferred_element_type=F32) for b in each]
    for b in each:
        decay = jnp.exp(jnp.where(incl, prep[b]["gdiff"], NEG_INF))
        prep[b]["x"] = -jnp.where(strict, kk[b] * decay, 0.0)
        prep[b]["aqk"] = jnp.where(incl, qk[b] * decay, 0.0)

    xs = [_split_bf16(prep[b]["x"]) for b in each]
    r = [prep[b]["r"] + _dot_split(xs[b], _split_bf16(prep[b]["r"])) for b in each]
    pw = [_dot_split(xs[b], xs[b]) for b in each]
    r = [r[b] + _dot_split(_split_bf16(pw[b]), _split_bf16(r[b])) for b in each]
    for _ in range(n_double - 1):
        pw = [jnp.dot(pw[b], pw[b], preferred_element_type=F32) for b in each]
        r = [r[b] + jnp.dot(pw[b], r[b], preferred_element_type=F32) for b in each]

    dot = lambda a, c: jnp.dot(a, c, preferred_element_type=F32)
    ws = [stack(lambda h, b=b: dot(head(r[b][:, HEAD_DIM_V:], h), s_olds[b][h])) for b in each]
    qs = [stack(lambda h, b=b: dot(head(prep[b]["qe"], h), s_olds[b][h])) for b in each]
    v_new = [r[b][:, :HEAD_DIM_V] - ws[b] for b in each]
    o = [qs[b] + dot(prep[b]["aqk"], v_new[b]) for b in each]
    s_news = [[s_olds[b][h] * jnp.exp(prep[b]["g_last"][h * C:h * C + 1, :])
               + lax.dot_general(head(prep[b]["kd"], h), head(v_new[b], h), TN_DIMS, preferred_element_type=F32)
               for h in range(H)] for b in each]
    o_outs = [(_rms(o[b], gn) * (prep[b]["zz"] * _sigmoid(prep[b]["zz"]))).astype(ob_ref.dtype) for b in each]

    for b in range(bb):
        for h in range(H):
            s_ref[b, h] = s_news[b][h]
            ob_ref[b, :, h * HEAD_DIM_V:(h + 1) * HEAD_DIM_V] = head(o_outs[b], h)

    @pl.when(c == n_c - 1)
    def _():
        sout_ref[...] = s_ref[...]


def _gdn(qkv, conv0, gates, z, s0, conv_w, alog_row, dtb_row, gn, t_valid, bb):
    b, t, _ = qkv.shape
    n_c = t // CHUNK
    blk = lambda w: pl.BlockSpec((bb, CHUNK, w), lambda i, c: (i, c, 0))
    per_b = lambda shape: pl.BlockSpec((bb,) + shape, lambda i, c: (i,) + (0,) * len(shape))
    state = (N_HEADS_GDN, HEAD_DIM_K, HEAD_DIM_V)
    return pl.pallas_call(
        functools.partial(_gdn_kernel, t_valid, bb),
        grid=(b // bb, n_c),
        in_specs=[blk(CONV_CH), per_b((CONV_W - 1, CONV_CH)), blk(LANES), blk(GDN_V), per_b(state),
                  _const_spec((CONV_W, CONV_CH)), _const_spec((1, LANES)), _const_spec((1, LANES)),
                  _const_spec((1, HEAD_DIM_V))],
        out_specs=(blk(GDN_V), per_b(state)),
        out_shape=(jax.ShapeDtypeStruct((b, t, GDN_V), BF16), jax.ShapeDtypeStruct((b,) + state, F32)),
        scratch_shapes=[pltpu.VMEM((bb, CHUNK + 2 * SUBLANES, CONV_CH), F32), pltpu.VMEM((bb,) + state, F32)],
        compiler_params=_params(2),
        name="gdn",
    )(qkv, conv0, gates, z, s0, conv_w, alog_row, dtb_row, gn)


def _topk_threshold(count_ge, qshape, n_sel):
    def bis(it, u):
        cu = u | lax.shift_left(jnp.int32(1), 31 - it)
        return jnp.where(count_ge(_key_to_float(cu ^ INT_MIN)) >= n_sel, cu, u)

    thr_key = lax.fori_loop(0, 32, bis, jnp.zeros(qshape, I32)) ^ INT_MIN
    return thr_key, _key_to_float(thr_key), _key_to_float(thr_key + 1)


def _topk_rule(count_ge, count_tie_le, qshape, n_sel, n_idx_bits, j_scr):
    thr_key, thr, nxt = _topk_threshold(count_ge, qshape, n_sel)
    need = n_sel - count_ge(nxt)
    excess = jnp.logical_and(count_ge(thr) > n_sel, thr_key > KEY_NEG)
    j_scr[...] = jnp.full(qshape, np.int32(2 ** 30), I32)

    @pl.when(jnp.max(excess.astype(I32)) > 0)
    def _():
        def jbis(it, j):
            cj = j | lax.shift_left(jnp.int32(1), n_idx_bits - 1 - it)
            return jnp.where(count_tie_le(thr, nxt, cj - 1) < need, cj, j)

        j_scr[...] = lax.fori_loop(0, n_idx_bits, jbis, jnp.zeros(qshape, I32))

    return thr, nxt, j_scr[...]


def _dsa_prompt_kernel(n_sel, tq, ck, t_total, qi_ref, ki2_ref, gate_ref, q_ref, k_ref, vt_ref,
                       o_ref, sc_scr, s_scr_even, s_scr_odd):
    s_scrs = (s_scr_even, s_scr_odd)
    i = pl.program_id(1)
    n_chunks = lax.div(i * tq + tq + ck - 1, ck)
    grp = ck // SUBLANES
    g3 = (grp, SUBLANES, tq)
    q_id = i * tq + lax.broadcasted_iota(I32, g3, 2)
    k_in_chunk = lax.broadcasted_iota(I32, g3, 0) * SUBLANES + lax.broadcasted_iota(I32, g3, 1)
    lane_lo = lax.broadcasted_iota(I32, (tq, LANES), 1) < HEAD_DIM_ATT
    rep8 = lambda v: jnp.broadcast_to(v, (SUBLANES, tq))

    qi = qi_ref[...] * (IDX_DIM ** -0.5)
    wi_t = gate_ref[...].T[GATE_WI:GATE_WI + N_IDX_HEADS, :] * (N_IDX_HEADS ** -0.5)
    qi_heads = []
    for h in range(N_IDX_HEADS):
        pair = qi[:, (h // 2) * LANES:(h // 2 + 1) * LANES]
        keep = lane_lo if h % 2 == 0 else jnp.logical_not(lane_lo)
        qi_heads.append(jnp.where(keep, pair, jnp.zeros_like(pair)))

    def score_body(c, carry):
        k0 = pl.multiple_of(c * ck, ck)
        kic = ki2_ref[pl.ds(k0, ck), :]
        sc = jnp.zeros((ck, tq), F32)
        for h in range(N_IDX_HEADS):
            logit = lax.dot_general(kic, qi_heads[h], NT_DIMS, preferred_element_type=F32)
            sc = sc + jnp.maximum(logit, 0.0) * wi_t[h:h + 1, :]
        causal = k0 + k_in_chunk <= q_id
        sc_scr[c] = jnp.where(causal, sc.reshape(g3), NEG_INF).reshape(ck, tq)
        return carry

    lax.fori_loop(0, n_chunks, score_body, 0)

    acc_shape = (N_ACC, SUBLANES, tq)
    fold = lambda x, op: op(x.reshape(grp // N_ACC, N_ACC, SUBLANES, tq), axis=0)
    finish = lambda acc, op: op(op(acc, axis=0), axis=0, keepdims=True)

    def count_ge(cand):
        cand8 = rep8(cand)
        body = lambda c, acc: acc + fold(jnp.where(sc_scr[c].reshape(g3) >= cand8, 1.0, 0.0), jnp.sum)
        return finish(lax.fori_loop(0, n_chunks, body, jnp.zeros(acc_shape, F32)), jnp.sum)

    _, thr, nxt = _topk_threshold(count_ge, (1, tq), n_sel)
    need = n_sel - count_ge(nxt)
    tri = (lax.broadcasted_iota(I32, (ck, ck), 0) >= lax.broadcasted_iota(I32, (ck, ck), 1)).astype(BF16)

    def mask_body(c, seen):
        s = sc_scr[c]
        above = s >= nxt
        tie = jnp.logical_and(s >= thr, jnp.logical_not(above))
        rank = seen + jnp.dot(tri, jnp.where(tie, 1.0, 0.0).astype(BF16), preferred_element_type=F32)
        sel = jnp.logical_or(above, jnp.logical_and(tie, rank <= need))
        sel = jnp.logical_and(sel, s > NEG_INF)
        sc_scr[c] = jnp.where(sel, 0.0, NEG_INF)
        return rank[ck - 1:ck, :]

    lax.fori_loop(0, n_chunks, mask_body, jnp.zeros((1, tq), F32))

    def q_masked(h):
        qp = q_ref[:, (h // 2) * LANES:(h // 2 + 1) * LANES] * (HEAD_DIM_ATT ** -0.5)
        keep = lane_lo if h % 2 == 0 else jnp.logical_not(lane_lo)
        return jnp.where(keep, qp, jnp.zeros_like(qp))

    sub = LANES
    fold_sub = lambda x, op: op(x.reshape(sub // (N_ACC * SUBLANES), N_ACC, SUBLANES, tq), axis=0)

    def s_pass(h, qm, c, macc):
        k0 = pl.multiple_of(c * ck, ck)
        for r in range(0, ck, sub):
            kc = k_ref[pl.ds(k0 + r, sub), (h // 2) * LANES:(h // 2 + 1) * LANES]
            st = lax.dot_general(kc, qm, NT_DIMS, preferred_element_type=F32) + sc_scr[c, r:r + sub, :]
            s_scrs[h % 2][c, r:r + sub, :] = st
            macc = jnp.maximum(macc, fold_sub(st, jnp.max))
        return macc

    def pv_pass(h, m, c, lacc, oacc):
        for r in range(0, ck, sub):
            pt = jnp.exp(s_scrs[h % 2][c, r:r + sub, :] - m)
            vt = vt_ref[c, h * HEAD_DIM_ATT:(h + 1) * HEAD_DIM_ATT, r:r + sub]
            lacc = lacc + fold_sub(pt, jnp.sum)
            oacc = oacc + jnp.dot(vt, pt.astype(BF16), preferred_element_type=F32)
        return lacc, oacc

    neg = jnp.full(acc_shape, NEG_INF, F32)
    zero_l = jnp.zeros(acc_shape, F32)
    zero_o = jnp.zeros((HEAD_DIM_ATT, tq), F32)
    qm0 = q_masked(0)
    macc = lax.fori_loop(0, n_chunks, lambda c, a: s_pass(0, qm0, c, a), neg)
    o_rows = []
    for h in range(N_HEADS_ATT):
        m = finish(macc, jnp.max)
        if h + 1 < N_HEADS_ATT:
            qm_next = q_masked(h + 1)

            def body(c, carry, h=h, m=m, qm_next=qm_next):
                lacc, oacc, macc_next = carry
                macc_next = s_pass(h + 1, qm_next, c, macc_next)
                lacc, oacc = pv_pass(h, m, c, lacc, oacc)
                return lacc, oacc, macc_next

            lacc, oacc, macc = lax.fori_loop(0, n_chunks, body, (zero_l, zero_o, neg))
        else:
            lacc, oacc = lax.fori_loop(0, n_chunks, lambda c, cr, h=h, m=m: pv_pass(h, m, c, *cr), (zero_l, zero_o))
        o_rows.append(oacc / finish(lacc, jnp.sum))
    o_ref[...] = jnp.concatenate(o_rows, axis=0).T.astype(o_ref.dtype)


def _dsa_prompt(qib, ki2b, gates, qab, kab, vtb, batch, t, n_sel, tq, ck):
    n_q = t // tq
    tile = lambda w: pl.BlockSpec((tq, w), lambda b, i: (b * n_q + i, 0))
    seq = lambda w: pl.BlockSpec((t, w), lambda b, i: (b, 0))
    return pl.pallas_call(
        functools.partial(_dsa_prompt_kernel, n_sel, tq, ck, t),
        grid=(batch, n_q),
        in_specs=[tile(IDX_W), seq(2 * IDX_DIM), tile(LANES), tile(ATT_W), seq(ATT_W),
                  pl.BlockSpec((None, t // ck, ATT_W, ck), lambda b, i: (b, 0, 0, 0))],
        out_specs=tile(ATT_W),
        out_shape=jax.ShapeDtypeStruct((batch * t, ATT_W), BF16),
        scratch_shapes=[pltpu.VMEM((t // ck, ck, tq), F32), pltpu.VMEM((t // ck, ck, tq), F32),
                        pltpu.VMEM((t // ck, ck, tq), F32)],
        compiler_params=_params(2),
        name="dsa_prompt",
    )(qib, ki2b, gates, qab, kab, vtb)


def _dsa_sample_kernel(n_sel, n_pages, pg, layer, pt_ref, qi_ref, wcol_ref, kinew_ref, q_ref, knew_ref, vnew_ref,
                       cki_hbm, ck_hbm, cv_hbm, o_ref, ki_buf, kv_buf, sem_ki, sem_kv, sc_scr, s_scr, j_scr):
    b = pl.program_id(0)
    n_groups = n_pages // pg
    n_ch = n_pages + 1
    tq = SUBLANES
    page = LANES
    qshape = (tq, page)
    lane = lax.broadcasted_iota(I32, qshape, 1)

    def ki_copy(p):
        return pltpu.make_async_copy(cki_hbm.at[layer, pt_ref[b, p]], ki_buf.at[p], sem_ki.at[0])

    def kv_copy(src, g, p, slot):
        return pltpu.make_async_copy(src.at[layer, pt_ref[b, g * pg + p]], kv_buf.at[slot, p], sem_kv.at[slot])

    def start_group(src, g, slot):
        for p in range(pg):
            kv_copy(src, g, p, slot).start()

    def wait_group(src, g, slot):
        for p in range(pg):
            kv_copy(src, g, p, slot).wait()

    for p in range(n_pages):
        ki_copy(p).start()
    start_group(ck_hbm, 0, 0)

    def new_t(x):
        wdt = x.shape[1]
        wp = max(wdt, page)
        if wp > wdt:
            x = jnp.concatenate([x, jnp.zeros((tq, wp - wdt), F32)], axis=1)
        sq = jnp.concatenate([x, jnp.zeros((page - tq, wp), F32)], axis=0)
        return jnp.concatenate([sq[:, t * page:(t + 1) * page].T for t in range(wp // page)], axis=0)[:wdt]

    qi = qi_ref[...] * (IDX_DIM ** -0.5)
    wcol = wcol_ref[...] * (N_IDX_HEADS ** -0.5)

    def score_chunk(ki_t):
        logit = jnp.dot(qi, ki_t.astype(BF16), preferred_element_type=F32)
        lw = jnp.maximum(logit, 0.0) * wcol
        sc = lw[0:tq]
        for h in range(1, N_IDX_HEADS):
            sc = sc + lw[h * tq:(h + 1) * tq]
        return sc

    sc_new = score_chunk(new_t(kinew_ref[...]))
    sc_scr[n_pages] = jnp.where(lane <= lax.broadcasted_iota(I32, qshape, 0), sc_new, NEG_INF)
    for p in range(n_pages):
        ki_copy(p).wait()
    for p in range(n_pages):
        sc_scr[p] = score_chunk(ki_buf[p])

    def row_total(hits):
        accs = [jnp.zeros(qshape, F32)] * N_ACC
        for c in range(n_ch):
            accs[c % N_ACC] = accs[c % N_ACC] + jnp.where(hits(c), 1.0, 0.0)
        tot = functools.reduce(lambda x, y: x + y, accs)
        return jnp.broadcast_to(jnp.sum(tot, axis=1, keepdims=True), qshape)

    def count_ge(cand):
        return row_total(lambda c: sc_scr[c] >= cand)

    def tie(sc, thr, nxt, c, j):
        return jnp.logical_and(jnp.logical_and(sc >= thr, jnp.logical_not(sc >= nxt)), c * page + lane <= j)

    def count_tie_le(thr, nxt, j):
        return row_total(lambda c: tie(sc_scr[c], thr, nxt, c, j))

    thr, nxt, jmax = _topk_rule(count_ge, count_tie_le, qshape, n_sel, int(np.ceil(np.log2(n_ch * page))), j_scr)

    def bias_body(c, carry):
        sc = sc_scr[c]
        sel = jnp.logical_or(sc >= nxt, tie(sc, thr, nxt, c, jmax))
        sel = jnp.logical_and(sel, sc > NEG_INF)
        sc_scr[c] = jnp.where(sel, 0.0, NEG_INF)
        return carry

    lax.fori_loop(0, n_ch, bias_body, 0)

    q = q_ref[...] * (HEAD_DIM_ATT ** -0.5)

    def s_chunk(kt_of_head, c):
        bias = sc_scr[c]
        parts = []
        for h in range(N_HEADS_ATT):
            qh = q[:, h * HEAD_DIM_ATT:(h + 1) * HEAD_DIM_ATT]
            parts.append(jnp.dot(qh, kt_of_head(h).astype(BF16), preferred_element_type=F32) + bias)
        sc = jnp.concatenate(parts, axis=0)
        s_scr[c] = sc
        return sc

    knew_t = new_t(knew_ref[...])
    macc = s_chunk(lambda h: knew_t[h * HEAD_DIM_ATT:(h + 1) * HEAD_DIM_ATT], n_pages)

    def k_body(g, macc):
        slot = lax.rem(g, 2)

        @pl.when(g + 1 < n_groups)
        def _():
            start_group(ck_hbm, g + 1, 1 - slot)

        @pl.when(g + 1 == n_groups)
        def _():
            start_group(cv_hbm, 0, 1 - slot)

        wait_group(ck_hbm, g, slot)
        for p in range(pg):
            macc = jnp.maximum(macc, s_chunk(lambda h, p=p: kv_buf[slot, p, h], g * pg + p))
        return macc

    macc = lax.fori_loop(0, n_groups, k_body, macc)
    m = jnp.max(macc, axis=1, keepdims=True)

    def pv_chunk(vt_of_head, c):
        pch = jnp.exp(s_scr[c] - m)
        pb = pch.astype(BF16)
        outs = [lax.dot_general(pb[h * tq:(h + 1) * tq], vt_of_head(h).astype(BF16), NT_DIMS,
                                preferred_element_type=F32) for h in range(N_HEADS_ATT)]
        return pch, jnp.concatenate(outs, axis=0)

    vnew_t = new_t(vnew_ref[...])
    lacc, oacc = pv_chunk(lambda h: vnew_t[h * HEAD_DIM_ATT:(h + 1) * HEAD_DIM_ATT], n_pages)

    def v_body(g, carry):
        lacc, oacc = carry
        slot = lax.rem(n_groups + g, 2)

        @pl.when(g + 1 < n_groups)
        def _():
            start_group(cv_hbm, g + 1, 1 - slot)

        wait_group(cv_hbm, g, slot)
        for p in range(pg):
            lp, op = pv_chunk(lambda h, p=p: kv_buf[slot, p, h], g * pg + p)
            lacc = lacc + lp
            oacc = oacc + op
        return lacc, oacc

    lacc, oacc = lax.fori_loop(0, n_groups, v_body, (lacc, oacc))
    o = oacc / jnp.sum(lacc, axis=1, keepdims=True)
    o_ref[...] = jnp.concatenate([o[h * tq:(h + 1) * tq] for h in range(N_HEADS_ATT)], axis=1)


def _dsa_sample(layer, page_table, qi, wcol, ki_new, q, k_new, v_new, cache_ki_t, cache_k_t, cache_v_t, n_sel, pg):
    b, n_pages = page_table.shape
    page = cache_k_t.shape[-1]
    assert page == LANES and n_pages % pg == 0 and ki_new.shape[1] == SUBLANES
    per_b = lambda shape: pl.BlockSpec((None,) + shape, lambda i, pt: (i,) + (0,) * len(shape))
    hbm = pl.BlockSpec(memory_space=pl.ANY)
    hq_idx = N_IDX_HEADS * SUBLANES
    hq = N_HEADS_ATT * SUBLANES
    grid_spec = pltpu.PrefetchScalarGridSpec(
        num_scalar_prefetch=1,
        grid=(b,),
        in_specs=[per_b((hq_idx, IDX_DIM)), per_b((hq_idx, 1)), per_b((SUBLANES, IDX_DIM)), per_b((SUBLANES, ATT_W)),
                  per_b((SUBLANES, ATT_W)), per_b((SUBLANES, ATT_W)), hbm, hbm, hbm],
        out_specs=per_b((SUBLANES, ATT_W)),
        scratch_shapes=[pltpu.VMEM((n_pages, IDX_DIM, page), F32),
                        pltpu.VMEM((2, pg, N_HEADS_ATT, HEAD_DIM_ATT, page), F32),
                        pltpu.SemaphoreType.DMA((1,)), pltpu.SemaphoreType.DMA((2,)),
                        pltpu.VMEM((n_pages + 1, SUBLANES, LANES), F32), pltpu.VMEM((n_pages + 1, hq, LANES), F32),
                        pltpu.VMEM((SUBLANES, LANES), I32)],
    )
    return pl.pallas_call(
        functools.partial(_dsa_sample_kernel, n_sel, n_pages, pg, layer),
        grid_spec=grid_spec,
        out_shape=jax.ShapeDtypeStruct((b, SUBLANES, ATT_W), F32),
        compiler_params=_params(1),
        name="dsa_sample",
    )(page_table, qi, wcol, ki_new, q, k_new, v_new, cache_ki_t, cache_k_t, cache_v_t)


def _regroup_w_in(w):
    o_qi = 3 * ATT_W
    o_ki = o_qi + IDX_W
    o_wi = o_ki + IDX_DIM
    o_cv = o_wi + N_IDX_HEADS
    o_a = o_cv + CONV_CH
    o_b = o_a + N_HEADS_GDN
    o_z = o_b + N_HEADS_GDN
    gate_cols = jnp.concatenate([w[:, o_wi:o_cv], w[:, o_a:o_b], w[:, o_b:o_z]], axis=1)
    gate_cols = jnp.pad(gate_cols, ((0, 0), (0, LANES - gate_cols.shape[1])))
    out = jnp.concatenate([w[:, :o_qi], w[:, o_cv:o_a], w[:, o_z:o_z + GDN_V], w[:, o_qi:o_ki],
                           w[:, o_ki:o_wi], w[:, o_ki:o_wi], gate_cols], axis=1)
    assert out.shape[1] == PJ_W
    return out.astype(BF16)


def _gate_row(v):
    return jnp.pad(v.astype(F32), (GATE_A, LANES - GATE_A - v.shape[0]))[None, :]


def _tiles(bp, tp, bs, ts, n_pages):
    ck = min(512, tp)
    return dict(
        tm_p=ck,
        tm_s=min(256, bs * ts),
        tq=min(256, tp),
        ck=ck,
        bb_p=min(4, bp),
        bb_s=min(4, bs),
        pg=min(8, n_pages),
    )


def kernel(x_prompt, x_sample, cache_k, cache_v, cache_kidx, state_ssm, state_conv, page_table, norm_mix, w_in,
           conv_w, a_log, dt_bias, norm_gdn_out, w_out, norm_mlp, w_up, w_down, norm_final):
    depth = w_in.shape[0]
    bp, tp, d = x_prompt.shape
    bs, ts, _ = x_sample.shape
    page = cache_k.shape[2]
    past = page_table.shape[1] * page
    cfg = _tiles(bp, tp, bs, ts, page_table.shape[1])
    n_sel_p = min(TOP_K_MAX, tp // 4)
    n_sel_s = min(TOP_K_MAX, (past + ts) // 4)
    assert tp % CHUNK == 0 and CONV_W - 1 <= ts <= CHUNK
    assert cfg["tq"] >= n_sel_p and tp % cfg["tq"] == 0 and tp % cfg["ck"] == 0 and cfg["ck"] % cfg["tq"] == 0
    assert bp % cfg["bb_p"] == 0 and bs % cfg["bb_s"] == 0

    cache_k_t = jnp.transpose(cache_k, (0, 1, 3, 4, 2))
    cache_v_t = jnp.transpose(cache_v, (0, 1, 3, 4, 2))
    cache_ki_t = jnp.transpose(cache_kidx, (0, 1, 3, 2))

    yp = x_prompt.reshape(bp * tp, d)
    ys = x_sample.reshape(bs * ts, d)
    outs = {k: [] for k in ("kp", "vp", "kip", "sp", "cp", "ks", "vs", "kis", "ss", "cs")}
    for l in range(depth):
        w_pj = _regroup_w_in(w_in[l])
        g_mix = norm_mix[l][None, :]
        alog_row, dtb_row = _gate_row(a_log[l]), _gate_row(dt_bias[l])
        gn = norm_gdn_out[l][None, :]
        w_out_b, w_up_b, w_dn_b = w_out[l].astype(BF16), w_up[l].astype(BF16), w_down[l].astype(BF16)
        g_mlp = norm_mlp[l][None, :]
        g_fin = norm_final[None, :]
        last = l == depth - 1

        qab, kab, vtb, kat, vat, kit, qkv, z, qib, ki2b, gates = _inproj(yp, g_mix, w_pj, cfg["tm_p"], tp, True)
        oa = _dsa_prompt(qib, ki2b, gates, qab, kab, vtb, bp, tp, n_sel_p, cfg["tq"], cfg["ck"])
        ob, s_new = _gdn(qkv.reshape(bp, tp, CONV_CH), jnp.zeros((bp, CONV_W - 1, CONV_CH), F32),
                         gates.reshape(bp, tp, LANES), z.reshape(bp, tp, GDN_V),
                         jnp.zeros((bp, N_HEADS_GDN, HEAD_DIM_K, HEAD_DIM_V), F32),
                         conv_w[l], alog_row, dtb_row, gn, CHUNK, cfg["bb_p"])
        outs["kp"].append(jnp.transpose(kat, (0, 3, 1, 2)))
        outs["vp"].append(jnp.transpose(vat, (0, 3, 1, 2)))
        outs["kip"].append(jnp.transpose(kit, (0, 2, 1)))
        outs["sp"].append(s_new.astype(state_ssm.dtype))
        outs["cp"].append(qkv.reshape(bp, tp, CONV_CH)[:, tp - (CONV_W - 1):, :])
        yp = _mlp_layer(yp, oa, ob.reshape(bp * tp, GDN_V), w_out_b, g_mlp, w_up_b, w_dn_b, g_fin, last, cfg["tm_p"])

        qab, ka, va, ki, qkv, z, qib, gates = _inproj(ys, g_mix, w_pj, cfg["tm_s"], ts, False)
        qi_rows = qib.reshape(bs, ts, N_IDX_HEADS, IDX_DIM).transpose(0, 2, 1, 3).reshape(bs, N_IDX_HEADS * ts, IDX_DIM)
        wcol = gates[:, GATE_WI:GATE_WI + N_IDX_HEADS].reshape(bs, ts, N_IDX_HEADS)
        wcol = wcol.transpose(0, 2, 1).reshape(bs, N_IDX_HEADS * ts, 1)
        oa = _dsa_sample(l, page_table, qi_rows, wcol, ki.reshape(bs, ts, IDX_DIM), qab.reshape(bs, ts, ATT_W),
                         ka.reshape(bs, ts, ATT_W), va.reshape(bs, ts, ATT_W),
                         cache_ki_t, cache_k_t, cache_v_t, n_sel_s, cfg["pg"])
        pad_t = lambda a: jnp.pad(a.reshape(bs, ts, a.shape[-1]), ((0, 0), (0, CHUNK - ts), (0, 0)))
        ob, s_new = _gdn(pad_t(qkv), state_conv[l], pad_t(gates), pad_t(z), state_ssm[l].astype(F32),
                         conv_w[l], alog_row, dtb_row, gn, ts, cfg["bb_s"])
        outs["ks"].append(ka.reshape(bs, ts, N_HEADS_ATT, HEAD_DIM_ATT))
        outs["vs"].append(va.reshape(bs, ts, N_HEADS_ATT, HEAD_DIM_ATT))
        outs["kis"].append(ki.reshape(bs, ts, IDX_DIM))
        outs["ss"].append(s_new.astype(state_ssm.dtype))
        outs["cs"].append(qkv.reshape(bs, ts, CONV_CH)[:, ts - (CONV_W - 1):, :])
        ys = _mlp_layer(ys, oa.reshape(bs * ts, ATT_W).astype(BF16), ob[:, :ts, :].reshape(bs * ts, GDN_V),
                        w_out_b, g_mlp, w_up_b, w_dn_b, g_fin, last, cfg["tm_s"])

    st = lambda k: jnp.stack(outs[k])
    return (yp.reshape(bp, tp, d), ys.reshape(bs, ts, d),
            st("kp"), st("vp"), st("kip"), st("sp"), st("cp"),
            st("ks"), st("vs"), st("kis"), st("ss"), st("cs"))
```

```python
import functools

import numpy as np
import jax
import jax.numpy as jnp
from jax import lax
from jax.experimental import pallas as pl
from jax.experimental.pallas import tpu as pltpu

F32 = jnp.float32
BF16 = jnp.bfloat16
I32 = jnp.int32
HI = lax.Precision.HIGHEST

EPS = 1e-6
N_HEADS_ATT = 8
HEAD_DIM_ATT = 64
ATT_W = N_HEADS_ATT * HEAD_DIM_ATT
N_IDX_HEADS = 4
IDX_DIM = 64
IDX_W = N_IDX_HEADS * IDX_DIM
TOP_K_MAX = 256
N_HEADS_GDN = 4
HEAD_DIM_K = 128
HEAD_DIM_V = 128
GDN_K = N_HEADS_GDN * HEAD_DIM_K
GDN_V = N_HEADS_GDN * HEAD_DIM_V
CONV_W = 4
CONV_CH = 2 * GDN_K + GDN_V
CHUNK = 64

LANES = 128
SUBLANES = 8
VMEM_LIMIT = 56 * 1024 * 1024
N_ACC = 4

PJ_ATT = 0
PJ_CV = PJ_ATT + 3 * ATT_W
PJ_Z = PJ_CV + CONV_CH
PJ_QI = PJ_Z + GDN_V
PJ_KI = PJ_QI + IDX_W
PJ_GATE = PJ_KI + 2 * IDX_DIM
PJ_W = PJ_GATE + LANES
GATE_WI = 0
GATE_A = 4
GATE_B = 8

INT_MIN = np.int32(-2 ** 31)
KEY_NEG = np.int32(np.uint32(0x807FFFFF).view(np.int32))
NEG_INF = float("-inf")

NT_DIMS = (((1,), (1,)), ((), ()))
TN_DIMS = (((0,), (0,)), ((), ()))


def _key_to_float(key):
    bits = jnp.where(key < 0, key ^ np.int32(0x7FFFFFFF), key)
    return jnp.where(key < KEY_NEG, NEG_INF, lax.bitcast_convert_type(bits, F32))


def _sigmoid(x):
    return 1.0 / (1.0 + jnp.exp(-x))


def _softplus(x):
    return jnp.maximum(x, 0.0) + jnp.log1p(jnp.exp(-jnp.abs(x)))


def _split_bf16(x):
    hi = x.astype(BF16)
    return hi, (x - hi.astype(F32)).astype(BF16)


def _dot_split(a, b):
    dot = lambda p, q: jnp.dot(p, q, preferred_element_type=F32)
    return dot(a[0], b[0]) + (dot(a[0], b[1]) + dot(a[1], b[0]))


def _rms(x, g):
    return x * lax.rsqrt(jnp.mean(x * x, axis=-1, keepdims=True) + EPS) * g


def _const_spec(shape):
    return pl.BlockSpec(shape, lambda *_: (0,) * len(shape), pipeline_mode=pl.Buffered(1))


def _params(n_grid_dims):
    return pltpu.CompilerParams(dimension_semantics=("arbitrary",) * n_grid_dims, vmem_limit_bytes=VMEM_LIMIT)


def _inproj_kernel(cache_t, x_ref, g_ref, w_ref, *out_refs):
    hb = _rms(x_ref[...], g_ref[...]).astype(BF16)
    proj = jnp.dot(hb, w_ref[...], preferred_element_type=F32)
    qa = proj[:, PJ_ATT:PJ_ATT + ATT_W]
    ka = proj[:, PJ_ATT + ATT_W:PJ_ATT + 2 * ATT_W]
    va = proj[:, PJ_ATT + 2 * ATT_W:PJ_ATT + 3 * ATT_W]
    qkv = proj[:, PJ_CV:PJ_CV + CONV_CH]
    z = proj[:, PJ_Z:PJ_Z + GDN_V]
    qi = proj[:, PJ_QI:PJ_QI + IDX_W]
    ki2 = proj[:, PJ_KI:PJ_KI + 2 * IDX_DIM]
    gates = proj[:, PJ_GATE:PJ_GATE + LANES]
    if cache_t:
        (qab_ref, kab_ref, vtb_ref, kat_ref, vat_ref, kit_ref, qkv_ref, z_ref, qib_ref, ki2b_ref, gate_ref) = out_refs
        tm = ka.shape[0]
        kab_ref[...] = ka.astype(BF16)
        ki2b_ref[...] = ki2.astype(BF16)
        vat = va.T
        vtb_ref[...] = vat.astype(BF16)
        kat_ref[...] = ka.T.reshape(N_HEADS_ATT, HEAD_DIM_ATT, tm)
        vat_ref[...] = vat.reshape(N_HEADS_ATT, HEAD_DIM_ATT, tm)
        kit_ref[...] = ki2.T[:IDX_DIM, :]
    else:
        (qab_ref, ka_ref, va_ref, ki_ref, qkv_ref, z_ref, qib_ref, gate_ref) = out_refs
        ka_ref[...] = ka
        va_ref[...] = va
        ki_ref[...] = ki2[:, :IDX_DIM]
    qab_ref[...] = qa.astype(BF16)
    qkv_ref[...] = qkv
    z_ref[...] = z
    qib_ref[...] = qi.astype(BF16)
    gate_ref[...] = gates


def _inproj(x2d, g, w, tm, seq_len, cache_t):
    n, d = x2d.shape
    row = lambda w_: pl.BlockSpec((tm, w_), lambda i: (i, 0))
    sds = jax.ShapeDtypeStruct
    common = [(row(CONV_CH), sds((n, CONV_CH), F32)), (row(GDN_V), sds((n, GDN_V), F32)),
              (row(IDX_W), sds((n, IDX_W), BF16)), (row(LANES), sds((n, LANES), F32))]
    if cache_t:
        assert seq_len % tm == 0
        n_t = seq_len // tm
        b = n // seq_len
        tok = lambda *lead: pl.BlockSpec((None,) + lead + (tm,), lambda i: (i // n_t,) + (0,) * len(lead) + (i % n_t,))
        head_t = (N_HEADS_ATT, HEAD_DIM_ATT)
        out = [(row(ATT_W), sds((n, ATT_W), BF16)),
               (row(ATT_W), sds((n, ATT_W), BF16)),
               (pl.BlockSpec((None, None, ATT_W, tm), lambda i: (i // n_t, i % n_t, 0, 0)),
                sds((b, n_t, ATT_W, tm), BF16)),
               (tok(*head_t), sds((b,) + head_t + (seq_len,), F32)),
               (tok(*head_t), sds((b,) + head_t + (seq_len,), F32)),
               (tok(IDX_DIM), sds((b, IDX_DIM, seq_len), F32)),
               common[0], common[1], common[2],
               (row(2 * IDX_DIM), sds((n, 2 * IDX_DIM), BF16)),
               common[3]]
    else:
        out = [(row(ATT_W), sds((n, ATT_W), BF16)), (row(ATT_W), sds((n, ATT_W), F32)),
               (row(ATT_W), sds((n, ATT_W), F32)), (row(IDX_DIM), sds((n, IDX_DIM), F32))] + common
    return pl.pallas_call(
        functools.partial(_inproj_kernel, cache_t),
        grid=(n // tm,),
        in_specs=[row(d), _const_spec((1, d)), _const_spec(w.shape)],
        out_specs=tuple(o[0] for o in out),
        out_shape=tuple(o[1] for o in out),
        compiler_params=_params(1),
        name="inproj",
    )(x2d, g, w)


def _mlp_kernel(ff_slab, final_norm, x_ref, oa_ref, ob_ref, wout_ref, gm_ref, wup_ref, wdn_ref, gf_ref, y_ref):
    mix = jnp.concatenate([oa_ref[...], ob_ref[...]], axis=1)
    x1 = x_ref[...] + jnp.dot(mix, wout_ref[...], preferred_element_type=F32)
    hm = _rms(x1, gm_ref[...]).astype(BF16)
    acc = x1
    d_ff = wup_ref.shape[1]
    for s in range(0, d_ff, ff_slab):
        up = jnp.dot(hm, wup_ref[:, s:s + ff_slab], preferred_element_type=F32)
        act = jnp.square(jnp.maximum(up, 0.0)).astype(BF16)
        acc = acc + jnp.dot(act, wdn_ref[s:s + ff_slab, :], preferred_element_type=F32)
    y_ref[...] = _rms(acc, gf_ref[...]) if final_norm else acc


def _mlp_layer(x2d, oa, ob, w_out, g_mlp, w_up, w_dn, g_final, final_norm, tm):
    n, d = x2d.shape
    row = lambda w: pl.BlockSpec((tm, w), lambda i: (i, 0))
    return pl.pallas_call(
        functools.partial(_mlp_kernel, 1024, final_norm),
        grid=(n // tm,),
        in_specs=[row(d), row(ATT_W), row(GDN_V), _const_spec(w_out.shape), _const_spec((1, d)),
                  _const_spec(w_up.shape), _const_spec(w_dn.shape), _const_spec((1, d))],
        out_specs=row(d),
        out_shape=jax.ShapeDtypeStruct((n, d), F32),
        compiler_params=_params(1),
        name="outproj_mlp",
    )(x2d, oa, ob, w_out, g_mlp, w_up, w_dn, g_final)


def _gdn_kernel(t_valid, bb, qkv_ref, conv0_ref, gate_ref, z_ref, s0_ref, cw_ref, alog_ref, dtb_ref, gn_ref,
                ob_ref, sout_ref, ext_ref, s_ref):
    c = pl.program_id(1)
    n_c = pl.num_programs(1)
    C = CHUNK
    HIST = SUBLANES

    @pl.when(c == 0)
    def _():
        for b in range(bb):
            ext_ref[b, 0:HIST, :] = jnp.zeros((HIST, CONV_CH), F32)
            ext_ref[b, HIST - (CONV_W - 1):HIST, :] = conv0_ref[b]
        s_ref[...] = s0_ref[...]

    H = N_HEADS_GDN
    R = H * C
    rows = lax.broadcasted_iota(I32, (R, R), 0)
    cols = lax.broadcasted_iota(I32, (R, R), 1)
    same_head = (rows // C) == (cols // C)
    incl = jnp.logical_and(same_head, rows >= cols)
    strict = jnp.logical_and(same_head, rows > cols)
    l_incl = (lax.broadcasted_iota(I32, (C, C), 0) >= lax.broadcasted_iota(I32, (C, C), 1)).astype(F32)
    gn = gn_ref[...]
    n_double = int(np.log2(C)) - 1
    stack = lambda f: jnp.concatenate([f(h) for h in range(H)], axis=0)
    head = lambda x, h: x[h * C:(h + 1) * C]

    convs, s_olds = [], []
    for b in range(bb):
        ext_ref[b, HIST:HIST + C, :] = qkv_ref[b]
        conv = ext_ref[b, HIST - 3:HIST - 3 + C, :] * cw_ref[0:1, :]
        for j in range(1, CONV_W):
            conv = conv + ext_ref[b, HIST - 3 + j:HIST - 3 + j + C, :] * cw_ref[j:j + 1, :]
        convs.append(conv * _sigmoid(conv))
        ext_ref[b, 0:HIST, :] = ext_ref[b, C:C + HIST, :]
        s_olds.append([s_ref[b, h] for h in range(H)])

    prep = []
    for b in range(bb):
        conv = convs[b]
        gates = gate_ref[b]
        g_slab = -jnp.exp(alog_ref[...]) * _softplus(gates + dtb_ref[...])
        beta_slab = _sigmoid(gates)
        if t_valid < C:
            live = lax.broadcasted_iota(I32, (C, LANES), 0) < t_valid
            g_slab = jnp.where(live, g_slab, 0.0)
            beta_slab = jnp.where(live, beta_slab, 0.0)
        gc_slab = jnp.dot(l_incl, g_slab, precision=HI, preferred_element_type=F32)
        gc_rows = gc_slab.T
        z = z_ref[b]

        q = stack(lambda h: conv[:, h * HEAD_DIM_K:(h + 1) * HEAD_DIM_K])
        k = stack(lambda h: conv[:, GDN_K + h * HEAD_DIM_K:GDN_K + (h + 1) * HEAD_DIM_K])
        v = stack(lambda h: conv[:, 2 * GDN_K + h * HEAD_DIM_V:2 * GDN_K + (h + 1) * HEAD_DIM_V])
        zz = stack(lambda h: z[:, h * HEAD_DIM_V:(h + 1) * HEAD_DIM_V])
        q = q * lax.rsqrt(jnp.sum(q * q, axis=-1, keepdims=True) + EPS) * (HEAD_DIM_K ** -0.5)
        k = k * lax.rsqrt(jnp.sum(k * k, axis=-1, keepdims=True) + EPS)
        gc = stack(lambda h: gc_slab[:, GATE_A + h:GATE_A + h + 1])
        beta = stack(lambda h: beta_slab[:, GATE_B + h:GATE_B + h + 1])
        gc_row = jnp.concatenate([gc_rows[GATE_A + h:GATE_A + h + 1, :] for h in range(H)], axis=1)
        g_last = stack(lambda h: jnp.broadcast_to(gc_slab[C - 1:C, GATE_A + h:GATE_A + h + 1], (C, 1)))
        eg = jnp.exp(gc)
        kb = k * beta
        vb = v * beta
        prep.append(dict(q=q, k=k, kb=kb, gdiff=gc - gc_row, r=jnp.concatenate([vb, kb * eg], axis=1), qe=q * eg,
                         kd=k * jnp.exp(g_last - gc), g_last=g_last, zz=zz))

    each = range(bb)
    kk = [lax.dot_general(prep[b]["kb"], prep[b]["k"], NT_DIMS, preferred_element_type=F32) for b in each]
    qk = [lax.dot_general(prep[b]["q"], prep[b]["k"], NT_DIMS, preferred_element_type=F32) for b in each]
    for b in each:
        decay = jnp.exp(jnp.where(incl, prep[b]["gdiff"], NEG_INF))
        prep[b]["x"] = -jnp.where(strict, kk[b] * decay, 0.0)
        prep[b]["aqk"] = jnp.where(incl, qk[b] * decay, 0.0)

    xs = [_split_bf16(prep[b]["x"]) for b in each]
    r = [prep[b]["r"] + _dot_split(xs[b], _split_bf16(prep[b]["r"])) for b in each]
    pw = [_dot_split(xs[b], xs[b]) for b in each]
    r = [r[b] + _dot_split(_split_bf16(pw[b]), _split_bf16(r[b])) for b in each]
    for _ in range(n_double - 1):
        pw = [jnp.dot(pw[b], pw[b], preferred_element_type=F32) for b in each]
        r = [r[b] + jnp.dot(pw[b], r[b], preferred_element_type=F32) for b in each]

    dot = lambda a, c: jnp.dot(a, c, preferred_element_type=F32)
    ws = [stack(lambda h, b=b: dot(head(r[b][:, HEAD_DIM_V:], h), s_olds[b][h])) for b in each]
    qs = [stack(lambda h, b=b: dot(head(prep[b]["qe"], h), s_olds[b][h])) for b in each]
    v_new = [r[b][:, :HEAD_DIM_V] - ws[b] for b in each]
    o = [qs[b] + dot(prep[b]["aqk"], v_new[b]) for b in each]
    s_news = [[s_olds[b][h] * jnp.exp(prep[b]["g_last"][h * C:h * C + 1, :])
               + lax.dot_general(head(prep[b]["kd"], h), head(v_new[b], h), TN_DIMS, preferred_element_type=F32)
               for h in range(H)] for b in each]
    o_outs = [(_rms(o[b], gn) * (prep[b]["zz"] * _sigmoid(prep[b]["zz"]))).astype(ob_ref.dtype) for b in each]

    for b in range(bb):
        for h in range(H):
            s_ref[b, h] = s_news[b][h]
            ob_ref[b, :, h * HEAD_DIM_V:(h + 1) * HEAD_DIM_V] = head(o_outs[b], h)

    @pl.when(c == n_c - 1)
    def _():
        sout_ref[...] = s_ref[...]


def _gdn(qkv, conv0, gates, z, s0, conv_w, alog_row, dtb_row, gn, t_valid, bb):
    b, t, _ = qkv.shape
    n_c = t // CHUNK
    blk = lambda w: pl.BlockSpec((bb, CHUNK, w), lambda i, c: (i, c, 0))
    per_b = lambda shape: pl.BlockSpec((bb,) + shape, lambda i, c: (i,) + (0,) * len(shape))
    state = (N_HEADS_GDN, HEAD_DIM_K, HEAD_DIM_V)
    return pl.pallas_call(
        functools.partial(_gdn_kernel, t_valid, bb),
        grid=(b // bb, n_c),
        in_specs=[blk(CONV_CH), per_b((CONV_W - 1, CONV_CH)), blk(LANES), blk(GDN_V), per_b(state),
                  _const_spec((CONV_W, CONV_CH)), _const_spec((1, LANES)), _const_spec((1, LANES)),
                  _const_spec((1, HEAD_DIM_V))],
        out_specs=(blk(GDN_V), per_b(state)),
        out_shape=(jax.ShapeDtypeStruct((b, t, GDN_V), BF16), jax.ShapeDtypeStruct((b,) + state, F32)),
        scratch_shapes=[pltpu.VMEM((bb, CHUNK + 2 * SUBLANES, CONV_CH), F32), pltpu.VMEM((bb,) + state, F32)],
        compiler_params=_params(2),
        name="gdn",
    )(qkv, conv0, gates, z, s0, conv_w, alog_row, dtb_row, gn)


def _topk_threshold(count_ge, qshape, n_sel):
    def bis(it, u):
        cu = u | lax.shift_left(jnp.int32(1), 31 - it)
        return jnp.where(count_ge(_key_to_float(cu ^ INT_MIN)) >= n_sel, cu, u)

    thr_key = lax.fori_loop(0, 32, bis, jnp.zeros(qshape, I32)) ^ INT_MIN
    return thr_key, _key_to_float(thr_key), _key_to_float(thr_key + 1)


def _topk_rule(count_ge, count_tie_le, qshape, n_sel, n_idx_bits, j_scr):
    thr_key, thr, nxt = _topk_threshold(count_ge, qshape, n_sel)
    need = n_sel - count_ge(nxt)
    excess = jnp.logical_and(count_ge(thr) > n_sel, thr_key > KEY_NEG)
    j_scr[...] = jnp.full(qshape, np.int32(2 ** 30), I32)

    @pl.when(jnp.max(excess.astype(I32)) > 0)
    def _():
        def jbis(it, j):
            cj = j | lax.shift_left(jnp.int32(1), n_idx_bits - 1 - it)
            return jnp.where(count_tie_le(thr, nxt, cj - 1) < need, cj, j)

        j_scr[...] = lax.fori_loop(0, n_idx_bits, jbis, jnp.zeros(qshape, I32))

    return thr, nxt, j_scr[...]


def _dsa_prompt_kernel(n_sel, tq, ck, t_total, qi_ref, ki2_ref, gate_ref, q_ref, k_ref, vt_ref,
                       o_ref, sc_scr, s_scr_even, s_scr_odd):
    s_scrs = (s_scr_even, s_scr_odd)
    i = pl.program_id(1)
    n_chunks = lax.div(i * tq + tq + ck - 1, ck)
    grp = ck // SUBLANES
    g3 = (grp, SUBLANES, tq)
    q_id = i * tq + lax.broadcasted_iota(I32, g3, 2)
    k_in_chunk = lax.broadcasted_iota(I32, g3, 0) * SUBLANES + lax.broadcasted_iota(I32, g3, 1)
    lane_lo = lax.broadcasted_iota(I32, (tq, LANES), 1) < HEAD_DIM_ATT
    rep8 = lambda v: jnp.broadcast_to(v, (SUBLANES, tq))

    qi = qi_ref[...] * (IDX_DIM ** -0.5)
    wi_t = gate_ref[...].T[GATE_WI:GATE_WI + N_IDX_HEADS, :] * (N_IDX_HEADS ** -0.5)
    qi_heads = []
    for h in range(N_IDX_HEADS):
        pair = qi[:, (h // 2) * LANES:(h // 2 + 1) * LANES]
        keep = lane_lo if h % 2 == 0 else jnp.logical_not(lane_lo)
        qi_heads.append(jnp.where(keep, pair, jnp.zeros_like(pair)))

    def score_body(c, carry):
        k0 = pl.multiple_of(c * ck, ck)
        kic = ki2_ref[pl.ds(k0, ck), :]
        sc = jnp.zeros((ck, tq), F32)
        for h in range(N_IDX_HEADS):
            logit = lax.dot_general(kic, qi_heads[h], NT_DIMS, preferred_element_type=F32)
            sc = sc + jnp.maximum(logit, 0.0) * wi_t[h:h + 1, :]
        causal = k0 + k_in_chunk <= q_id
        sc_scr[c] = jnp.where(causal, sc.reshape(g3), NEG_INF).reshape(ck, tq)
        return carry

    lax.fori_loop(0, n_chunks, score_body, 0)

    acc_shape = (N_ACC, SUBLANES, tq)
    fold = lambda x, op: op(x.reshape(grp // N_ACC, N_ACC, SUBLANES, tq), axis=0)
    finish = lambda acc, op: op(op(acc, axis=0), axis=0, keepdims=True)

    def count_ge(cand):
        cand8 = rep8(cand)
        body = lambda c, acc: acc + fold(jnp.where(sc_scr[c].reshape(g3) >= cand8, 1.0, 0.0), jnp.sum)
        return finish(lax.fori_loop(0, n_chunks, body, jnp.zeros(acc_shape, F32)), jnp.sum)

    _, thr, nxt = _topk_threshold(count_ge, (1, tq), n_sel)
    need = n_sel - count_ge(nxt)
    tri = (lax.broadcasted_iota(I32, (ck, ck), 0) >= lax.broadcasted_iota(I32, (ck, ck), 1)).astype(BF16)

    def mask_body(c, seen):
        s = sc_scr[c]
        above = s >= nxt
        tie = jnp.logical_and(s >= thr, jnp.logical_not(above))
        rank = seen + jnp.dot(tri, jnp.where(tie, 1.0, 0.0).astype(BF16), preferred_element_type=F32)
        sel = jnp.logical_or(above, jnp.logical_and(tie, rank <= need))
        sel = jnp.logical_and(sel, s > NEG_INF)
        sc_scr[c] = jnp.where(sel, 0.0, NEG_INF)
        return rank[ck - 1:ck, :]

    lax.fori_loop(0, n_chunks, mask_body, jnp.zeros((1, tq), F32))

    def q_masked(h):
        qp = q_ref[:, (h // 2) * LANES:(h // 2 + 1) * LANES] * (HEAD_DIM_ATT ** -0.5)
        keep = lane_lo if h % 2 == 0 else jnp.logical_not(lane_lo)
        return jnp.where(keep, qp, jnp.zeros_like(qp))

    sub = LANES
    fold_sub = lambda x, op: op(x.reshape(sub // (N_ACC * SUBLANES), N_ACC, SUBLANES, tq), axis=0)

    def s_pass(h, qm, c, macc):
        k0 = pl.multiple_of(c * ck, ck)
        for r in range(0, ck, sub):
            kc = k_ref[pl.ds(k0 + r, sub), (h // 2) * LANES:(h // 2 + 1) * LANES]
            st = lax.dot_general(kc, qm, NT_DIMS, preferred_element_type=F32) + sc_scr[c, r:r + sub, :]
            s_scrs[h % 2][c, r:r + sub, :] = st
            macc = jnp.maximum(macc, fold_sub(st, jnp.max))
        return macc

    def pv_pass(h, m, c, lacc, oacc):
        for r in range(0, ck, sub):
            pt = jnp.exp(s_scrs[h % 2][c, r:r + sub, :] - m)
            vt = vt_ref[c, h * HEAD_DIM_ATT:(h + 1) * HEAD_DIM_ATT, r:r + sub]
            lacc = lacc + fold_sub(pt, jnp.sum)
            oacc = oacc + jnp.dot(vt, pt.astype(BF16), preferred_element_type=F32)
        return lacc, oacc

    neg = jnp.full(acc_shape, NEG_INF, F32)
    zero_l = jnp.zeros(acc_shape, F32)
    zero_o = jnp.zeros((HEAD_DIM_ATT, tq), F32)
    qm0 = q_masked(0)
    macc = lax.fori_loop(0, n_chunks, lambda c, a: s_pass(0, qm0, c, a), neg)
    o_rows = []
    for h in range(N_HEADS_ATT):
        m = finish(macc, jnp.max)
        if h + 1 < N_HEADS_ATT:
            qm_next = q_masked(h + 1)

            def body(c, carry, h=h, m=m, qm_next=qm_next):
                lacc, oacc, macc_next = carry
                macc_next = s_pass(h + 1, qm_next, c, macc_next)
                lacc, oacc = pv_pass(h, m, c, lacc, oacc)
                return lacc, oacc, macc_next

            lacc, oacc, macc = lax.fori_loop(0, n_chunks, body, (zero_l, zero_o, neg))
        else:
            lacc, oacc = lax.fori_loop(0, n_chunks, lambda c, cr, h=h, m=m: pv_pass(h, m, c, *cr), (zero_l, zero_o))
        o_rows.append(oacc / finish(lacc, jnp.sum))
    o_ref[...] = jnp.concatenate(o_rows, axis=0).T.astype(o_ref.dtype)


def _dsa_prompt(qib, ki2b, gates, qab, kab, vtb, batch, t, n_sel, tq, ck):
    n_q = t // tq
    tile = lambda w: pl.BlockSpec((tq, w), lambda b, i: (b * n_q + i, 0))
    seq = lambda w: pl.BlockSpec((t, w), lambda b, i: (b, 0))
    return pl.pallas_call(
        functools.partial(_dsa_prompt_kernel, n_sel, tq, ck, t),
        grid=(batch, n_q),
        in_specs=[tile(IDX_W), seq(2 * IDX_DIM), tile(LANES), tile(ATT_W), seq(ATT_W),
                  pl.BlockSpec((None, t // ck, ATT_W, ck), lambda b, i: (b, 0, 0, 0))],
        out_specs=tile(ATT_W),
        out_shape=jax.ShapeDtypeStruct((batch * t, ATT_W), BF16),
        scratch_shapes=[pltpu.VMEM((t // ck, ck, tq), F32), pltpu.VMEM((t // ck, ck, tq), F32),
                        pltpu.VMEM((t // ck, ck, tq), F32)],
        compiler_params=_params(2),
        name="dsa_prompt",
    )(qib, ki2b, gates, qab, kab, vtb)


def _dsa_sample_kernel(n_sel, n_pages, pg, layer, pt_ref, qi_ref, wcol_ref, kinew_ref, q_ref, knew_ref, vnew_ref,
                       cki_hbm, ck_hbm, cv_hbm, o_ref, ki_buf, kv_buf, sem_ki, sem_kv, sc_scr, s_scr, j_scr):
    b = pl.program_id(0)
    n_groups = n_pages // pg
    n_ch = n_pages + 1
    tq = SUBLANES
    page = LANES
    qshape = (tq, page)
    lane = lax.broadcasted_iota(I32, qshape, 1)

    def ki_copy(p):
        return pltpu.make_async_copy(cki_hbm.at[layer, pt_ref[b, p]], ki_buf.at[p], sem_ki.at[0])

    def kv_copy(src, g, p, slot):
        return pltpu.make_async_copy(src.at[layer, pt_ref[b, g * pg + p]], kv_buf.at[slot, p], sem_kv.at[slot])

    def start_group(src, g, slot):
        for p in range(pg):
            kv_copy(src, g, p, slot).start()

    def wait_group(src, g, slot):
        for p in range(pg):
            kv_copy(src, g, p, slot).wait()

    for p in range(n_pages):
        ki_copy(p).start()
    start_group(ck_hbm, 0, 0)

    def new_t(x):
        wdt = x.shape[1]
        wp = max(wdt, page)
        if wp > wdt:
            x = jnp.concatenate([x, jnp.zeros((tq, wp - wdt), F32)], axis=1)
        sq = jnp.concatenate([x, jnp.zeros((page - tq, wp), F32)], axis=0)
        return jnp.concatenate([sq[:, t * page:(t + 1) * page].T for t in range(wp // page)], axis=0)[:wdt]

    qi = qi_ref[...] * (IDX_DIM ** -0.5)
    wcol = wcol_ref[...] * (N_IDX_HEADS ** -0.5)

    def score_chunk(ki_t):
        logit = jnp.dot(qi, ki_t.astype(BF16), preferred_element_type=F32)
        lw = jnp.maximum(logit, 0.0) * wcol
        sc = lw[0:tq]
        for h in range(1, N_IDX_HEADS):
            sc = sc + lw[h * tq:(h + 1) * tq]
        return sc

    sc_new = score_chunk(new_t(kinew_ref[...]))
    sc_scr[n_pages] = jnp.where(lane <= lax.broadcasted_iota(I32, qshape, 0), sc_new, NEG_INF)
    for p in range(n_pages):
        ki_copy(p).wait()
    for p in range(n_pages):
        sc_scr[p] = score_chunk(ki_buf[p])

    def row_total(hits):
        accs = [jnp.zeros(qshape, F32)] * N_ACC
        for c in range(n_ch):
            accs[c % N_ACC] = accs[c % N_ACC] + jnp.where(hits(c), 1.0, 0.0)
        tot = functools.reduce(lambda x, y: x + y, accs)
        return jnp.broadcast_to(jnp.sum(tot, axis=1, keepdims=True), qshape)

    def count_ge(cand):
        return row_total(lambda c: sc_scr[c] >= cand)

    def tie(sc, thr, nxt, c, j):
        return jnp.logical_and(jnp.logical_and(sc >= thr, jnp.logical_not(sc >= nxt)), c * page + lane <= j)

    def count_tie_le(thr, nxt, j):
        return row_total(lambda c: tie(sc_scr[c], thr, nxt, c, j))

    thr, nxt, jmax = _topk_rule(count_ge, count_tie_le, qshape, n_sel, int(np.ceil(np.log2(n_ch * page))), j_scr)

    def bias_body(c, carry):
        sc = sc_scr[c]
        sel = jnp.logical_or(sc >= nxt, tie(sc, thr, nxt, c, jmax))
        sel = jnp.logical_and(sel, sc > NEG_INF)
        sc_scr[c] = jnp.where(sel, 0.0, NEG_INF)
        return carry

    lax.fori_loop(0, n_ch, bias_body, 0)

    q = q_ref[...] * (HEAD_DIM_ATT ** -0.5)

    def s_chunk(kt_of_head, c):
        bias = sc_scr[c]
        parts = []
        for h in range(N_HEADS_ATT):
            qh = q[:, h * HEAD_DIM_ATT:(h + 1) * HEAD_DIM_ATT]
            parts.append(jnp.dot(qh, kt_of_head(h).astype(BF16), preferred_element_type=F32) + bias)
        sc = jnp.concatenate(parts, axis=0)
        s_scr[c] = sc
        return sc

    knew_t = new_t(knew_ref[...])
    macc = s_chunk(lambda h: knew_t[h * HEAD_DIM_ATT:(h + 1) * HEAD_DIM_ATT], n_pages)

    def k_body(g, macc):
        slot = lax.rem(g, 2)

        @pl.when(g + 1 < n_groups)
        def _():
            start_group(ck_hbm, g + 1, 1 - slot)

        @pl.when(g + 1 == n_groups)
        def _():
            start_group(cv_hbm, 0, 1 - slot)

        wait_group(ck_hbm, g, slot)
        for p in range(0, pg, 2):
            c = g * pg + p
            bias = jnp.concatenate([sc_scr[c], sc_scr[c + 1]], axis=1)
            parts = []
            for h in range(N_HEADS_ATT):
                kt = jnp.concatenate([kv_buf[slot, p, h], kv_buf[slot, p + 1, h]], axis=1).astype(BF16)
                qh = q[:, h * HEAD_DIM_ATT:(h + 1) * HEAD_DIM_ATT]
                parts.append(jnp.dot(qh, kt, preferred_element_type=F32) + bias)
            sc = jnp.concatenate(parts, axis=0)
            s_scr[c] = sc[:, :page]
            s_scr[c + 1] = sc[:, page:]
            macc = jnp.maximum(macc, jnp.maximum(sc[:, :page], sc[:, page:]))
        return macc

    macc = lax.fori_loop(0, n_groups, k_body, macc)
    m = jnp.max(macc, axis=1, keepdims=True)

    def pv_chunk(vt_of_head, c):
        pch = jnp.exp(s_scr[c] - m)
        pb = pch.astype(BF16)
        outs = [lax.dot_general(pb[h * tq:(h + 1) * tq], vt_of_head(h).astype(BF16), NT_DIMS,
                                preferred_element_type=F32) for h in range(N_HEADS_ATT)]
        return pch, jnp.concatenate(outs, axis=0)

    vnew_t = new_t(vnew_ref[...])
    lacc, oacc = pv_chunk(lambda h: vnew_t[h * HEAD_DIM_ATT:(h + 1) * HEAD_DIM_ATT], n_pages)

    def v_body(g, carry):
        lacc, oacc = carry
        slot = lax.rem(n_groups + g, 2)

        @pl.when(g + 1 < n_groups)
        def _():
            start_group(cv_hbm, g + 1, 1 - slot)

        wait_group(cv_hbm, g, slot)
        for p in range(0, pg, 2):
            c = g * pg + p
            pch = jnp.exp(jnp.concatenate([s_scr[c], s_scr[c + 1]], axis=1) - m)
            pb = pch.astype(BF16)
            outs = []
            for h in range(N_HEADS_ATT):
                vt = jnp.concatenate([kv_buf[slot, p, h], kv_buf[slot, p + 1, h]], axis=1).astype(BF16)
                outs.append(lax.dot_general(pb[h * tq:(h + 1) * tq], vt, NT_DIMS, preferred_element_type=F32))
            lacc = lacc + (pch[:, :page] + pch[:, page:])
            oacc = oacc + jnp.concatenate(outs, axis=0)
        return lacc, oacc

    lacc, oacc = lax.fori_loop(0, n_groups, v_body, (lacc, oacc))
    o = oacc / jnp.sum(lacc, axis=1, keepdims=True)
    o_ref[...] = jnp.concatenate([o[h * tq:(h + 1) * tq] for h in range(N_HEADS_ATT)], axis=1)


def _dsa_sample(layer, page_table, qi, wcol, ki_new, q, k_new, v_new, cache_ki_t, cache_k_t, cache_v_t, n_sel, pg):
    b, n_pages = page_table.shape
    page = cache_k_t.shape[-1]
    assert page == LANES and n_pages % pg == 0 and ki_new.shape[1] == SUBLANES
    per_b = lambda shape: pl.BlockSpec((None,) + shape, lambda i, pt: (i,) + (0,) * len(shape))
    hbm = pl.BlockSpec(memory_space=pl.ANY)
    hq_idx = N_IDX_HEADS * SUBLANES
    hq = N_HEADS_ATT * SUBLANES
    grid_spec = pltpu.PrefetchScalarGridSpec(
        num_scalar_prefetch=1,
        grid=(b,),
        in_specs=[per_b((hq_idx, IDX_DIM)), per_b((hq_idx, 1)), per_b((SUBLANES, IDX_DIM)), per_b((SUBLANES, ATT_W)),
                  per_b((SUBLANES, ATT_W)), per_b((SUBLANES, ATT_W)), hbm, hbm, hbm],
        out_specs=per_b((SUBLANES, ATT_W)),
        scratch_shapes=[pltpu.VMEM((n_pages, IDX_DIM, page), F32),
                        pltpu.VMEM((2, pg, N_HEADS_ATT, HEAD_DIM_ATT, page), F32),
                        pltpu.SemaphoreType.DMA((1,)), pltpu.SemaphoreType.DMA((2,)),
                        pltpu.VMEM((n_pages + 1, SUBLANES, LANES), F32), pltpu.VMEM((n_pages + 1, hq, LANES), F32),
                        pltpu.VMEM((SUBLANES, LANES), I32)],
    )
    return pl.pallas_call(
        functools.partial(_dsa_sample_kernel, n_sel, n_pages, pg, layer),
        grid_spec=grid_spec,
        out_shape=jax.ShapeDtypeStruct((b, SUBLANES, ATT_W), F32),
        compiler_params=_params(1),
        name="dsa_sample",
    )(page_table, qi, wcol, ki_new, q, k_new, v_new, cache_ki_t, cache_k_t, cache_v_t)


def _regroup_w_in(w):
    o_qi = 3 * ATT_W
    o_ki = o_qi + IDX_W
    o_wi = o_ki + IDX_DIM
    o_cv = o_wi + N_IDX_HEADS
    o_a = o_cv + CONV_CH
    o_b = o_a + N_HEADS_GDN
    o_z = o_b + N_HEADS_GDN
    gate_cols = jnp.concatenate([w[:, o_wi:o_cv], w[:, o_a:o_b], w[:, o_b:o_z]], axis=1)
    gate_cols = jnp.pad(gate_cols, ((0, 0), (0, LANES - gate_cols.shape[1])))
    out = jnp.concatenate([w[:, :o_qi], w[:, o_cv:o_a], w[:, o_z:o_z + GDN_V], w[:, o_qi:o_ki],
                           w[:, o_ki:o_wi], w[:, o_ki:o_wi], gate_cols], axis=1)
    assert out.shape[1] == PJ_W
    return out.astype(BF16)


def _gate_row(v):
    return jnp.pad(v.astype(F32), (GATE_A, LANES - GATE_A - v.shape[0]))[None, :]


def _tiles(bp, tp, bs, ts, n_pages):
    ck = min(512, tp)
    return dict(
        tm_p=ck,
        tm_s=min(256, bs * ts),
        tq=min(256, tp),
        ck=ck,
        bb_p=min(4, bp),
        bb_s=min(4, bs),
        pg=min(16, n_pages),
    )


def kernel(x_prompt, x_sample, cache_k, cache_v, cache_kidx, state_ssm, state_conv, page_table, norm_mix, w_in,
           conv_w, a_log, dt_bias, norm_gdn_out, w_out, norm_mlp, w_up, w_down, norm_final):
    depth = w_in.shape[0]
    bp, tp, d = x_prompt.shape
    bs, ts, _ = x_sample.shape
    page = cache_k.shape[2]
    past = page_table.shape[1] * page
    cfg = _tiles(bp, tp, bs, ts, page_table.shape[1])
    n_sel_p = min(TOP_K_MAX, tp // 4)
    n_sel_s = min(TOP_K_MAX, (past + ts) // 4)
    assert tp % CHUNK == 0 and CONV_W - 1 <= ts <= CHUNK
    assert cfg["tq"] >= n_sel_p and tp % cfg["tq"] == 0 and tp % cfg["ck"] == 0 and cfg["ck"] % cfg["tq"] == 0
    assert bp % cfg["bb_p"] == 0 and bs % cfg["bb_s"] == 0

    cache_k_t = jnp.transpose(cache_k, (0, 1, 3, 4, 2))
    cache_v_t = jnp.transpose(cache_v, (0, 1, 3, 4, 2))
    cache_ki_t = jnp.transpose(cache_kidx, (0, 1, 3, 2))

    yp = x_prompt.reshape(bp * tp, d)
    ys = x_sample.reshape(bs * ts, d)
    outs = {k: [] for k in ("kp", "vp", "kip", "sp", "cp", "ks", "vs", "kis", "ss", "cs")}
    for l in range(depth):
        w_pj = _regroup_w_in(w_in[l])
        g_mix = norm_mix[l][None, :]
        alog_row, dtb_row = _gate_row(a_log[l]), _gate_row(dt_bias[l])
        gn = norm_gdn_out[l][None, :]
        w_out_b, w_up_b, w_dn_b = w_out[l].astype(BF16), w_up[l].astype(BF16), w_down[l].astype(BF16)
        g_mlp = norm_mlp[l][None, :]
        g_fin = norm_final[None, :]
        last = l == depth - 1

        qab, kab, vtb, kat, vat, kit, qkv, z, qib, ki2b, gates = _inproj(yp, g_mix, w_pj, cfg["tm_p"], tp, True)
        oa = _dsa_prompt(qib, ki2b, gates, qab, kab, vtb, bp, tp, n_sel_p, cfg["tq"], cfg["ck"])
        ob, s_new = _gdn(qkv.reshape(bp, tp, CONV_CH), jnp.zeros((bp, CONV_W - 1, CONV_CH), F32),
                         gates.reshape(bp, tp, LANES), z.reshape(bp, tp, GDN_V),
                         jnp.zeros((bp, N_HEADS_GDN, HEAD_DIM_K, HEAD_DIM_V), F32),
                         conv_w[l], alog_row, dtb_row, gn, CHUNK, cfg["bb_p"])
        outs["kp"].append(jnp.transpose(kat, (0, 3, 1, 2)))
        outs["vp"].append(jnp.transpose(vat, (0, 3, 1, 2)))
        outs["kip"].append(jnp.transpose(kit, (0, 2, 1)))
        outs["sp"].append(s_new.astype(state_ssm.dtype))
        outs["cp"].append(qkv.reshape(bp, tp, CONV_CH)[:, tp - (CONV_W - 1):, :])
        yp = _mlp_layer(yp, oa, ob.reshape(bp * tp, GDN_V), w_out_b, g_mlp, w_up_b, w_dn_b, g_fin, last, cfg["tm_p"])

        qab, ka, va, ki, qkv, z, qib, gates = _inproj(ys, g_mix, w_pj, cfg["tm_s"], ts, False)
        qi_rows = qib.reshape(bs, ts, N_IDX_HEADS, IDX_DIM).transpose(0, 2, 1, 3).reshape(bs, N_IDX_HEADS * ts, IDX_DIM)
        wcol = gates[:, GATE_WI:GATE_WI + N_IDX_HEADS].reshape(bs, ts, N_IDX_HEADS)
        wcol = wcol.transpose(0, 2, 1).reshape(bs, N_IDX_HEADS * ts, 1)
        oa = _dsa_sample(l, page_table, qi_rows, wcol, ki.reshape(bs, ts, IDX_DIM), qab.reshape(bs, ts, ATT_W),
                         ka.reshape(bs, ts, ATT_W), va.reshape(bs, ts, ATT_W),
                         cache_ki_t, cache_k_t, cache_v_t, n_sel_s, cfg["pg"])
        pad_t = lambda a: jnp.pad(a.reshape(bs, ts, a.shape[-1]), ((0, 0), (0, CHUNK - ts), (0, 0)))
        ob, s_new = _gdn(pad_t(qkv), state_conv[l], pad_t(gates), pad_t(z), state_ssm[l].astype(F32),
                         conv_w[l], alog_row, dtb_row, gn, ts, cfg["bb_s"])
        outs["ks"].append(ka.reshape(bs, ts, N_HEADS_ATT, HEAD_DIM_ATT))
        outs["vs"].append(va.reshape(bs, ts, N_HEADS_ATT, HEAD_DIM_ATT))
        outs["kis"].append(ki.reshape(bs, ts, IDX_DIM))
        outs["ss"].append(s_new.astype(state_ssm.dtype))
        outs["cs"].append(qkv.reshape(bs, ts, CONV_CH)[:, ts - (CONV_W - 1):, :])
        ys = _mlp_layer(ys, oa.reshape(bs * ts, ATT_W).astype(BF16), ob[:, :ts, :].reshape(bs * ts, GDN_V),
                        w_out_b, g_mlp, w_up_b, w_dn_b, g_fin, last, cfg["tm_s"])

    st = lambda k: jnp.stack(outs[k])
    return (yp.reshape(bp, tp, d), ys.reshape(bs, ts, d),
            st("kp"), st("vp"), st("kip"), st("sp"), st("cp"),
            st("ks"), st("vs"), st("kis"), st("ss"), st("cs"))
```

```python
import functools

import numpy as np
import jax
import jax.numpy as jnp
from jax import lax
from jax.experimental import pallas as pl
from jax.experimental.pallas import tpu as pltpu

F32 = jnp.float32
BF16 = jnp.bfloat16
I32 = jnp.int32
HI = lax.Precision.HIGHEST

EPS = 1e-6
N_HEADS_ATT = 8
HEAD_DIM_ATT = 64
ATT_W = N_HEADS_ATT * HEAD_DIM_ATT
N_IDX_HEADS = 4
IDX_DIM = 64
IDX_W = N_IDX_HEADS * IDX_DIM
TOP_K_MAX = 256
N_HEADS_GDN = 4
HEAD_DIM_K = 128
HEAD_DIM_V = 128
GDN_K = N_HEADS_GDN * HEAD_DIM_K
GDN_V = N_HEADS_GDN * HEAD_DIM_V
CONV_W = 4
CONV_CH = 2 * GDN_K + GDN_V
CHUNK = 64

LANES = 128
SUBLANES = 8
VMEM_LIMIT = 56 * 1024 * 1024
N_ACC = 4

PJ_ATT = 0
PJ_CV = PJ_ATT + 3 * ATT_W
PJ_Z = PJ_CV + CONV_CH
PJ_QI = PJ_Z + GDN_V
PJ_KI = PJ_QI + IDX_W
PJ_GATE = PJ_KI + 2 * IDX_DIM
PJ_W = PJ_GATE + LANES
GATE_WI = 0
GATE_A = 4
GATE_B = 8

INT_MIN = np.int32(-2 ** 31)
KEY_NEG = np.int32(np.uint32(0x807FFFFF).view(np.int32))
NEG_INF = float("-inf")

NT_DIMS = (((1,), (1,)), ((), ()))
TN_DIMS = (((0,), (0,)), ((), ()))


def _key_to_float(key):
    bits = jnp.where(key < 0, key ^ np.int32(0x7FFFFFFF), key)
    return jnp.where(key < KEY_NEG, NEG_INF, lax.bitcast_convert_type(bits, F32))


def _sigmoid(x):
    return 1.0 / (1.0 + jnp.exp(-x))


def _softplus(x):
    return jnp.maximum(x, 0.0) + jnp.log1p(jnp.exp(-jnp.abs(x)))


def _split_bf16(x):
    hi = x.astype(BF16)
    return hi, (x - hi.astype(F32)).astype(BF16)


def _dot_split(a, b):
    dot = lambda p, q: jnp.dot(p, q, preferred_element_type=F32)
    return dot(a[0], b[0]) + (dot(a[0], b[1]) + dot(a[1], b[0]))


def _rms(x, g):
    return x * lax.rsqrt(jnp.mean(x * x, axis=-1, keepdims=True) + EPS) * g


def _const_spec(shape):
    return pl.BlockSpec(shape, lambda *_: (0,) * len(shape), pipeline_mode=pl.Buffered(1))


def _params(n_grid_dims):
    return pltpu.CompilerParams(dimension_semantics=("arbitrary",) * n_grid_dims, vmem_limit_bytes=VMEM_LIMIT)


def _inproj_kernel(cache_t, x_ref, g_ref, w_ref, *out_refs):
    hb = _rms(x_ref[...], g_ref[...]).astype(BF16)
    proj = jnp.dot(hb, w_ref[...], preferred_element_type=F32)
    qa = proj[:, PJ_ATT:PJ_ATT + ATT_W]
    ka = proj[:, PJ_ATT + ATT_W:PJ_ATT + 2 * ATT_W]
    va = proj[:, PJ_ATT + 2 * ATT_W:PJ_ATT + 3 * ATT_W]
    qkv = proj[:, PJ_CV:PJ_CV + CONV_CH]
    z = proj[:, PJ_Z:PJ_Z + GDN_V]
    qi = proj[:, PJ_QI:PJ_QI + IDX_W]
    ki2 = proj[:, PJ_KI:PJ_KI + 2 * IDX_DIM]
    gates = proj[:, PJ_GATE:PJ_GATE + LANES]
    if cache_t:
        (qab_ref, kab_ref, vtb_ref, kat_ref, vat_ref, kit_ref, qkv_ref, z_ref, qib_ref, ki2b_ref, gate_ref) = out_refs
        tm = ka.shape[0]
        kab_ref[...] = ka.astype(BF16)
        ki2b_ref[...] = ki2.astype(BF16)
        vat = va.T
        vtb_ref[...] = vat.astype(BF16)
        kat_ref[...] = ka.T.reshape(N_HEADS_ATT, HEAD_DIM_ATT, tm)
        vat_ref[...] = vat.reshape(N_HEADS_ATT, HEAD_DIM_ATT, tm)
        kit_ref[...] = ki2.T[:IDX_DIM, :]
    else:
        (qab_ref, ka_ref, va_ref, ki_ref, qkv_ref, z_ref, qib_ref, gate_ref) = out_refs
        ka_ref[...] = ka
        va_ref[...] = va
        ki_ref[...] = ki2[:, :IDX_DIM]
    qab_ref[...] = qa.astype(BF16)
    qkv_ref[...] = qkv
    z_ref[...] = z
    qib_ref[...] = qi.astype(BF16)
    gate_ref[...] = gates


def _inproj(x2d, g, w, tm, seq_len, cache_t):
    n, d = x2d.shape
    row = lambda w_: pl.BlockSpec((tm, w_), lambda i: (i, 0))
    sds = jax.ShapeDtypeStruct
    common = [(row(CONV_CH), sds((n, CONV_CH), F32)), (row(GDN_V), sds((n, GDN_V), F32)),
              (row(IDX_W), sds((n, IDX_W), BF16)), (row(LANES), sds((n, LANES), F32))]
    if cache_t:
        assert seq_len % tm == 0
        n_t = seq_len // tm
        b = n // seq_len
        tok = lambda *lead: pl.BlockSpec((None,) + lead + (tm,), lambda i: (i // n_t,) + (0,) * len(lead) + (i % n_t,))
        head_t = (N_HEADS_ATT, HEAD_DIM_ATT)
        out = [(row(ATT_W), sds((n, ATT_W), BF16)),
               (row(ATT_W), sds((n, ATT_W), BF16)),
               (pl.BlockSpec((None, None, ATT_W, tm), lambda i: (i // n_t, i % n_t, 0, 0)),
                sds((b, n_t, ATT_W, tm), BF16)),
               (tok(*head_t), sds((b,) + head_t + (seq_len,), F32)),
               (tok(*head_t), sds((b,) + head_t + (seq_len,), F32)),
               (tok(IDX_DIM), sds((b, IDX_DIM, seq_len), F32)),
               common[0], common[1], common[2],
               (row(2 * IDX_DIM), sds((n, 2 * IDX_DIM), BF16)),
               common[3]]
    else:
        out = [(row(ATT_W), sds((n, ATT_W), BF16)), (row(ATT_W), sds((n, ATT_W), F32)),
               (row(ATT_W), sds((n, ATT_W), F32)), (row(IDX_DIM), sds((n, IDX_DIM), F32))] + common
    return pl.pallas_call(
        functools.partial(_inproj_kernel, cache_t),
        grid=(n // tm,),
        in_specs=[row(d), _const_spec((1, d)), _const_spec(w.shape)],
        out_specs=tuple(o[0] for o in out),
        out_shape=tuple(o[1] for o in out),
        compiler_params=_params(1),
        name="inproj",
    )(x2d, g, w)


def _mlp_kernel(ff_slab, final_norm, x_ref, oa_ref, ob_ref, wout_ref, gm_ref, wup_ref, wdn_ref, gf_ref, y_ref):
    mix = jnp.concatenate([oa_ref[...], ob_ref[...]], axis=1)
    x1 = x_ref[...] + jnp.dot(mix, wout_ref[...], preferred_element_type=F32)
    hm = _rms(x1, gm_ref[...]).astype(BF16)
    acc = x1
    d_ff = wup_ref.shape[1]
    for s in range(0, d_ff, ff_slab):
        up = jnp.dot(hm, wup_ref[:, s:s + ff_slab], preferred_element_type=F32)
        act = jnp.square(jnp.maximum(up, 0.0)).astype(BF16)
        acc = acc + jnp.dot(act, wdn_ref[s:s + ff_slab, :], preferred_element_type=F32)
    y_ref[...] = _rms(acc, gf_ref[...]) if final_norm else acc


def _mlp_layer(x2d, oa, ob, w_out, g_mlp, w_up, w_dn, g_final, final_norm, tm):
    n, d = x2d.shape
    row = lambda w: pl.BlockSpec((tm, w), lambda i: (i, 0))
    return pl.pallas_call(
        functools.partial(_mlp_kernel, 1024, final_norm),
        grid=(n // tm,),
        in_specs=[row(d), row(ATT_W), row(GDN_V), _const_spec(w_out.shape), _const_spec((1, d)),
                  _const_spec(w_up.shape), _const_spec(w_dn.shape), _const_spec((1, d))],
        out_specs=row(d),
        out_shape=jax.ShapeDtypeStruct((n, d), F32),
        compiler_params=_params(1),
        name="outproj_mlp",
    )(x2d, oa, ob, w_out, g_mlp, w_up, w_dn, g_final)


def _gdn_kernel(t_valid, bb, qkv_ref, conv0_ref, gate_ref, z_ref, s0_ref, cw_ref, alog_ref, dtb_ref, gn_ref,
                ob_ref, sout_ref, ext_ref, s_ref):
    c = pl.program_id(1)
    n_c = pl.num_programs(1)
    C = CHUNK
    HIST = SUBLANES

    @pl.when(c == 0)
    def _():
        for b in range(bb):
            ext_ref[b, 0:HIST, :] = jnp.zeros((HIST, CONV_CH), F32)
            ext_ref[b, HIST - (CONV_W - 1):HIST, :] = conv0_ref[b]
        s_ref[...] = s0_ref[...]

    H = N_HEADS_GDN
    R = H * C
    rows = lax.broadcasted_iota(I32, (R, R), 0)
    cols = lax.broadcasted_iota(I32, (R, R), 1)
    same_head = (rows // C) == (cols // C)
    incl = jnp.logical_and(same_head, rows >= cols)
    strict = jnp.logical_and(same_head, rows > cols)
    l_incl = (lax.broadcasted_iota(I32, (C, C), 0) >= lax.broadcasted_iota(I32, (C, C), 1)).astype(F32)
    gn = gn_ref[...]
    n_double = int(np.log2(C)) - 1
    stack = lambda f: jnp.concatenate([f(h) for h in range(H)], axis=0)
    head = lambda x, h: x[h * C:(h + 1) * C]

    convs, s_olds = [], []
    for b in range(bb):
        ext_ref[b, HIST:HIST + C, :] = qkv_ref[b]
        conv = ext_ref[b, HIST - 3:HIST - 3 + C, :] * cw_ref[0:1, :]
        for j in range(1, CONV_W):
            conv = conv + ext_ref[b, HIST - 3 + j:HIST - 3 + j + C, :] * cw_ref[j:j + 1, :]
        convs.append(conv * _sigmoid(conv))
        ext_ref[b, 0:HIST, :] = ext_ref[b, C:C + HIST, :]
        s_olds.append([s_ref[b, h] for h in range(H)])

    prep = []
    for b in range(bb):
        conv = convs[b]
        gates = gate_ref[b]
        g_slab = -jnp.exp(alog_ref[...]) * _softplus(gates + dtb_ref[...])
        beta_slab = _sigmoid(gates)
        if t_valid < C:
            live = lax.broadcasted_iota(I32, (C, LANES), 0) < t_valid
            g_slab = jnp.where(live, g_slab, 0.0)
            beta_slab = jnp.where(live, beta_slab, 0.0)
        gc_slab = jnp.dot(l_incl, g_slab, precision=HI, preferred_element_type=F32)
        gc_rows = gc_slab.T
        z = z_ref[b]

        q = stack(lambda h: conv[:, h * HEAD_DIM_K:(h + 1) * HEAD_DIM_K])
        k = stack(lambda h: conv[:, GDN_K + h * HEAD_DIM_K:GDN_K + (h + 1) * HEAD_DIM_K])
        v = stack(lambda h: conv[:, 2 * GDN_K + h * HEAD_DIM_V:2 * GDN_K + (h + 1) * HEAD_DIM_V])
        zz = stack(lambda h: z[:, h * HEAD_DIM_V:(h + 1) * HEAD_DIM_V])
        q = q * lax.rsqrt(jnp.sum(q * q, axis=-1, keepdims=True) + EPS) * (HEAD_DIM_K ** -0.5)
        k = k * lax.rsqrt(jnp.sum(k * k, axis=-1, keepdims=True) + EPS)
        gc = stack(lambda h: gc_slab[:, GATE_A + h:GATE_A + h + 1])
        beta = stack(lambda h: beta_slab[:, GATE_B + h:GATE_B + h + 1])
        gc_row = jnp.concatenate([gc_rows[GATE_A + h:GATE_A + h + 1, :] for h in range(H)], axis=1)
        g_last = stack(lambda h: jnp.broadcast_to(gc_slab[C - 1:C, GATE_A + h:GATE_A + h + 1], (C, 1)))
        eg = jnp.exp(gc)
        kb = k * beta
        vb = v * beta
        prep.append(dict(q=q, k=k, kb=kb, gdiff=gc - gc_row, r=jnp.concatenate([vb, kb * eg], axis=1), qe=q * eg,
                         kd=k * jnp.exp(g_last - gc), g_last=g_last, zz=zz))

    each = range(bb)
    kk = [lax.dot_general(prep[b]["kb"], prep[b]["k"], NT_DIMS, preferred_element_type=F32) for b in each]
    qk = [lax.dot_general(prep[b]["q"], prep[b]["k"], NT_DIMS, preferred_element_type=F32) for b in each]
    for b in each:
        decay = jnp.exp(jnp.where(incl, prep[b]["gdiff"], NEG_INF))
        prep[b]["x"] = -jnp.where(strict, kk[b] * decay, 0.0)
        prep[b]["aqk"] = jnp.where(incl, qk[b] * decay, 0.0)

    xs = [_split_bf16(prep[b]["x"]) for b in each]
    r = [prep[b]["r"] + _dot_split(xs[b], _split_bf16(prep[b]["r"])) for b in each]
    pw = [_dot_split(xs[b], xs[b]) for b in each]
    r = [r[b] + _dot_split(_split_bf16(pw[b]), _split_bf16(r[b])) for b in each]
    for _ in range(n_double - 1):
        pw = [jnp.dot(pw[b], pw[b], preferred_element_type=F32) for b in each]
        r = [r[b] + jnp.dot(pw[b], r[b], preferred_element_type=F32) for b in each]

    dot = lambda a, c: jnp.dot(a, c, preferred_element_type=F32)
    ws = [stack(lambda h, b=b: dot(head(r[b][:, HEAD_DIM_V:], h), s_olds[b][h])) for b in each]
    qs = [stack(lambda h, b=b: dot(head(prep[b]["qe"], h), s_olds[b][h])) for b in each]
    v_new = [r[b][:, :HEAD_DIM_V] - ws[b] for b in each]
    o = [qs[b] + dot(prep[b]["aqk"], v_new[b]) for b in each]
    s_news = [[s_olds[b][h] * jnp.exp(prep[b]["g_last"][h * C:h * C + 1, :])
               + lax.dot_general(head(prep[b]["kd"], h), head(v_new[b], h), TN_DIMS, preferred_element_type=F32)
               for h in range(H)] for b in each]
    o_outs = [(_rms(o[b], gn) * (prep[b]["zz"] * _sigmoid(prep[b]["zz"]))).astype(ob_ref.dtype) for b in each]

    for b in range(bb):
        for h in range(H):
            s_ref[b, h] = s_news[b][h]
            ob_ref[b, :, h * HEAD_DIM_V:(h + 1) * HEAD_DIM_V] = head(o_outs[b], h)

    @pl.when(c == n_c - 1)
    def _():
        sout_ref[...] = s_ref[...]


def _gdn(qkv, conv0, gates, z, s0, conv_w, alog_row, dtb_row, gn, t_valid, bb):
    b, t, _ = qkv.shape
    n_c = t // CHUNK
    blk = lambda w: pl.BlockSpec((bb, CHUNK, w), lambda i, c: (i, c, 0))
    per_b = lambda shape: pl.BlockSpec((bb,) + shape, lambda i, c: (i,) + (0,) * len(shape))
    state = (N_HEADS_GDN, HEAD_DIM_K, HEAD_DIM_V)
    return pl.pallas_call(
        functools.partial(_gdn_kernel, t_valid, bb),
        grid=(b // bb, n_c),
        in_specs=[blk(CONV_CH), per_b((CONV_W - 1, CONV_CH)), blk(LANES), blk(GDN_V), per_b(state),
                  _const_spec((CONV_W, CONV_CH)), _const_spec((1, LANES)), _const_spec((1, LANES)),
                  _const_spec((1, HEAD_DIM_V))],
        out_specs=(blk(GDN_V), per_b(state)),
        out_shape=(jax.ShapeDtypeStruct((b, t, GDN_V), BF16), jax.ShapeDtypeStruct((b,) + state, F32)),
        scratch_shapes=[pltpu.VMEM((bb, CHUNK + 2 * SUBLANES, CONV_CH), F32), pltpu.VMEM((bb,) + state, F32)],
        compiler_params=_params(2),
        name="gdn",
    )(qkv, conv0, gates, z, s0, conv_w, alog_row, dtb_row, gn)


def _topk_threshold(count_ge, qshape, n_sel):
    def bis(it, u):
        cu = u | lax.shift_left(jnp.int32(1), 31 - it)
        return jnp.where(count_ge(_key_to_float(cu ^ INT_MIN)) >= n_sel, cu, u)

    thr_key = lax.fori_loop(0, 32, bis, jnp.zeros(qshape, I32)) ^ INT_MIN
    return thr_key, _key_to_float(thr_key), _key_to_float(thr_key + 1)


def _topk_rule(count_ge, count_tie_le, qshape, n_sel, n_idx_bits, j_scr):
    thr_key, thr, nxt = _topk_threshold(count_ge, qshape, n_sel)
    need = n_sel - count_ge(nxt)
    excess = jnp.logical_and(count_ge(thr) > n_sel, thr_key > KEY_NEG)
    j_scr[...] = jnp.full(qshape, np.int32(2 ** 30), I32)

    @pl.when(jnp.max(excess.astype(I32)) > 0)
    def _():
        def jbis(it, j):
            cj = j | lax.shift_left(jnp.int32(1), n_idx_bits - 1 - it)
            return jnp.where(count_tie_le(thr, nxt, cj - 1) < need, cj, j)

        j_scr[...] = lax.fori_loop(0, n_idx_bits, jbis, jnp.zeros(qshape, I32))

    return thr, nxt, j_scr[...]


def _dsa_prompt_kernel(n_sel, tq, ck, t_total, qi_ref, ki2_ref, gate_ref, q_ref, k_ref, vt_ref,
                       o_ref, sc_scr, s_scr_even, s_scr_odd):
    s_scrs = (s_scr_even, s_scr_odd)
    i = pl.program_id(1)
    n_chunks = lax.div(i * tq + tq + ck - 1, ck)
    grp = ck // SUBLANES
    g3 = (grp, SUBLANES, tq)
    q_id = i * tq + lax.broadcasted_iota(I32, g3, 2)
    k_in_chunk = lax.broadcasted_iota(I32, g3, 0) * SUBLANES + lax.broadcasted_iota(I32, g3, 1)
    lane_lo = lax.broadcasted_iota(I32, (tq, LANES), 1) < HEAD_DIM_ATT
    rep8 = lambda v: jnp.broadcast_to(v, (SUBLANES, tq))

    qi = qi_ref[...] * (IDX_DIM ** -0.5)
    wi_t = gate_ref[...].T[GATE_WI:GATE_WI + N_IDX_HEADS, :] * (N_IDX_HEADS ** -0.5)
    qi_heads = []
    for h in range(N_IDX_HEADS):
        pair = qi[:, (h // 2) * LANES:(h // 2 + 1) * LANES]
        keep = lane_lo if h % 2 == 0 else jnp.logical_not(lane_lo)
        qi_heads.append(jnp.where(keep, pair, jnp.zeros_like(pair)))

    def score_body(c, carry):
        k0 = pl.multiple_of(c * ck, ck)
        kic = ki2_ref[pl.ds(k0, ck), :]
        sc = jnp.zeros((ck, tq), F32)
        for h in range(N_IDX_HEADS):
            logit = lax.dot_general(kic, qi_heads[h], NT_DIMS, preferred_element_type=F32)
            sc = sc + jnp.maximum(logit, 0.0) * wi_t[h:h + 1, :]
        causal = k0 + k_in_chunk <= q_id
        sc_scr[c] = jnp.where(causal, sc.reshape(g3), NEG_INF).reshape(ck, tq)
        return carry

    lax.fori_loop(0, n_chunks, score_body, 0)

    n_acc = max(1, N_ACC * LANES // tq)
    acc_shape = (n_acc, SUBLANES, tq)
    fold = lambda x, op: op(x.reshape(grp // n_acc, n_acc, SUBLANES, tq), axis=0)
    finish = lambda acc, op: op(op(acc, axis=0), axis=0, keepdims=True)

    def count_ge(cand):
        cand8 = rep8(cand)
        body = lambda c, acc: acc + fold(jnp.where(sc_scr[c].reshape(g3) >= cand8, 1.0, 0.0), jnp.sum)
        return finish(lax.fori_loop(0, n_chunks, body, jnp.zeros(acc_shape, F32)), jnp.sum)

    _, thr, nxt = _topk_threshold(count_ge, (1, tq), n_sel)
    need = n_sel - count_ge(nxt)
    tri = (lax.broadcasted_iota(I32, (ck, ck), 0) >= lax.broadcasted_iota(I32, (ck, ck), 1)).astype(BF16)

    def mask_body(c, seen):
        s = sc_scr[c]
        above = s >= nxt
        tie = jnp.logical_and(s >= thr, jnp.logical_not(above))
        rank = seen + jnp.dot(tri, jnp.where(tie, 1.0, 0.0).astype(BF16), preferred_element_type=F32)
        sel = jnp.logical_or(above, jnp.logical_and(tie, rank <= need))
        sel = jnp.logical_and(sel, s > NEG_INF)
        sc_scr[c] = jnp.where(sel, 0.0, NEG_INF)
        return rank[ck - 1:ck, :]

    lax.fori_loop(0, n_chunks, mask_body, jnp.zeros((1, tq), F32))

    def q_masked(h):
        qp = q_ref[:, (h // 2) * LANES:(h // 2 + 1) * LANES] * (HEAD_DIM_ATT ** -0.5)
        keep = lane_lo if h % 2 == 0 else jnp.logical_not(lane_lo)
        return jnp.where(keep, qp, jnp.zeros_like(qp))

    sub = LANES
    fold_sub = lambda x, op: op(x.reshape(sub // (n_acc * SUBLANES), n_acc, SUBLANES, tq), axis=0)

    def s_pass(h, qm, c, macc):
        k0 = pl.multiple_of(c * ck, ck)
        for r in range(0, ck, sub):
            kc = k_ref[pl.ds(k0 + r, sub), (h // 2) * LANES:(h // 2 + 1) * LANES]
            st = lax.dot_general(kc, qm, NT_DIMS, preferred_element_type=F32) + sc_scr[c, r:r + sub, :]
            s_scrs[h % 2][c, r:r + sub, :] = st
            macc = jnp.maximum(macc, fold_sub(st, jnp.max))
        return macc

    def pv_pass(h, m, c, lacc, oacc):
        for r in range(0, ck, sub):
            pt = jnp.exp(s_scrs[h % 2][c, r:r + sub, :] - m)
            vt = vt_ref[c, h * HEAD_DIM_ATT:(h + 1) * HEAD_DIM_ATT, r:r + sub]
            lacc = lacc + fold_sub(pt, jnp.sum)
            oacc = oacc + jnp.dot(vt, pt.astype(BF16), preferred_element_type=F32)
        return lacc, oacc

    neg = jnp.full(acc_shape, NEG_INF, F32)
    zero_l = jnp.zeros(acc_shape, F32)
    zero_o = jnp.zeros((HEAD_DIM_ATT, tq), F32)
    qm0 = q_masked(0)
    macc = lax.fori_loop(0, n_chunks, lambda c, a: s_pass(0, qm0, c, a), neg)
    o_rows = []
    for h in range(N_HEADS_ATT):
        m = finish(macc, jnp.max)
        if h + 1 < N_HEADS_ATT:
            qm_next = q_masked(h + 1)

            def body(c, carry, h=h, m=m, qm_next=qm_next):
                lacc, oacc, macc_next = carry
                macc_next = s_pass(h + 1, qm_next, c, macc_next)
                lacc, oacc = pv_pass(h, m, c, lacc, oacc)
                return lacc, oacc, macc_next

            lacc, oacc, macc = lax.fori_loop(0, n_chunks, body, (zero_l, zero_o, neg))
        else:
            lacc, oacc = lax.fori_loop(0, n_chunks, lambda c, cr, h=h, m=m: pv_pass(h, m, c, *cr), (zero_l, zero_o))
        o_rows.append(oacc / finish(lacc, jnp.sum))
    o_ref[...] = jnp.concatenate(o_rows, axis=0).T.astype(o_ref.dtype)


def _dsa_prompt(qib, ki2b, gates, qab, kab, vtb, batch, t, n_sel, tq, ck):
    n_q = t // tq
    tile = lambda w: pl.BlockSpec((tq, w), lambda b, i: (b * n_q + i, 0))
    seq = lambda w: pl.BlockSpec((t, w), lambda b, i: (b, 0))
    return pl.pallas_call(
        functools.partial(_dsa_prompt_kernel, n_sel, tq, ck, t),
        grid=(batch, n_q),
        in_specs=[tile(IDX_W), seq(2 * IDX_DIM), tile(LANES), tile(ATT_W), seq(ATT_W),
                  pl.BlockSpec((None, t // ck, ATT_W, ck), lambda b, i: (b, 0, 0, 0))],
        out_specs=tile(ATT_W),
        out_shape=jax.ShapeDtypeStruct((batch * t, ATT_W), BF16),
        scratch_shapes=[pltpu.VMEM((t // ck, ck, tq), F32), pltpu.VMEM((t // ck, ck, tq), F32),
                        pltpu.VMEM((t // ck, ck, tq), F32)],
        compiler_params=_params(2),
        name="dsa_prompt",
    )(qib, ki2b, gates, qab, kab, vtb)


def _dsa_sample_kernel(n_sel, n_pages, pg, layer, pt_ref, qi_ref, wcol_ref, kinew_ref, q_ref, knew_ref, vnew_ref,
                       cki_hbm, ck_hbm, cv_hbm, o_ref, ki_buf, kv_buf, sem_ki, sem_kv, sc_scr, s_scr, j_scr):
    b = pl.program_id(0)
    n_groups = n_pages // pg
    n_ch = n_pages + 1
    tq = SUBLANES
    page = LANES
    qshape = (tq, page)
    lane = lax.broadcasted_iota(I32, qshape, 1)

    def ki_copy(p):
        return pltpu.make_async_copy(cki_hbm.at[layer, pt_ref[b, p]], ki_buf.at[p], sem_ki.at[0])

    def kv_copy(src, g, p, slot):
        return pltpu.make_async_copy(src.at[layer, pt_ref[b, g * pg + p]], kv_buf.at[slot, p], sem_kv.at[slot])

    def start_group(src, g, slot):
        for p in range(pg):
            kv_copy(src, g, p, slot).start()

    def wait_group(src, g, slot):
        for p in range(pg):
            kv_copy(src, g, p, slot).wait()

    for p in range(n_pages):
        ki_copy(p).start()
    start_group(ck_hbm, 0, 0)

    def new_t(x):
        wdt = x.shape[1]
        wp = max(wdt, page)
        if wp > wdt:
            x = jnp.concatenate([x, jnp.zeros((tq, wp - wdt), F32)], axis=1)
        sq = jnp.concatenate([x, jnp.zeros((page - tq, wp), F32)], axis=0)
        return jnp.concatenate([sq[:, t * page:(t + 1) * page].T for t in range(wp // page)], axis=0)[:wdt]

    qi = qi_ref[...] * (IDX_DIM ** -0.5)
    wcol = wcol_ref[...] * (N_IDX_HEADS ** -0.5)

    def score_chunk(ki_t):
        logit = jnp.dot(qi, ki_t.astype(BF16), preferred_element_type=F32)
        lw = jnp.maximum(logit, 0.0) * wcol
        sc = lw[0:tq]
        for h in range(1, N_IDX_HEADS):
            sc = sc + lw[h * tq:(h + 1) * tq]
        return sc

    sc_new = score_chunk(new_t(kinew_ref[...]))
    sc_scr[n_pages] = jnp.where(lane <= lax.broadcasted_iota(I32, qshape, 0), sc_new, NEG_INF)
    for p in range(n_pages):
        ki_copy(p).wait()
    for p in range(n_pages):
        sc_scr[p] = score_chunk(ki_buf[p])

    def row_total(hits):
        accs = [jnp.zeros(qshape, F32)] * N_ACC
        for c in range(n_ch):
            accs[c % N_ACC] = accs[c % N_ACC] + jnp.where(hits(c), 1.0, 0.0)
        tot = functools.reduce(lambda x, y: x + y, accs)
        return jnp.broadcast_to(jnp.sum(tot, axis=1, keepdims=True), qshape)

    def count_ge(cand):
        return row_total(lambda c: sc_scr[c] >= cand)

    def tie(sc, thr, nxt, c, j):
        return jnp.logical_and(jnp.logical_and(sc >= thr, jnp.logical_not(sc >= nxt)), c * page + lane <= j)

    def count_tie_le(thr, nxt, j):
        return row_total(lambda c: tie(sc_scr[c], thr, nxt, c, j))

    thr, nxt, jmax = _topk_rule(count_ge, count_tie_le, qshape, n_sel, int(np.ceil(np.log2(n_ch * page))), j_scr)

    def bias_body(c, carry):
        sc = sc_scr[c]
        sel = jnp.logical_or(sc >= nxt, tie(sc, thr, nxt, c, jmax))
        sel = jnp.logical_and(sel, sc > NEG_INF)
        sc_scr[c] = jnp.where(sel, 0.0, NEG_INF)
        return carry

    lax.fori_loop(0, n_ch, bias_body, 0)

    q = q_ref[...] * (HEAD_DIM_ATT ** -0.5)

    def s_chunk(kt_of_head, c):
        bias = sc_scr[c]
        parts = []
        for h in range(N_HEADS_ATT):
            qh = q[:, h * HEAD_DIM_ATT:(h + 1) * HEAD_DIM_ATT]
            parts.append(jnp.dot(qh, kt_of_head(h).astype(BF16), preferred_element_type=F32) + bias)
        sc = jnp.concatenate(parts, axis=0)
        s_scr[c] = sc
        return sc

    knew_t = new_t(knew_ref[...])
    macc = s_chunk(lambda h: knew_t[h * HEAD_DIM_ATT:(h + 1) * HEAD_DIM_ATT], n_pages)

    def k_body(g, macc):
        slot = lax.rem(g, 2)

        @pl.when(g + 1 < n_groups)
        def _():
            start_group(ck_hbm, g + 1, 1 - slot)

        @pl.when(g + 1 == n_groups)
        def _():
            start_group(cv_hbm, 0, 1 - slot)

        wait_group(ck_hbm, g, slot)
        for p in range(0, pg, 2):
            c = g * pg + p
            bias = jnp.concatenate([sc_scr[c], sc_scr[c + 1]], axis=1)
            parts = []
            for h in range(N_HEADS_ATT):
                kt = jnp.concatenate([kv_buf[slot, p, h], kv_buf[slot, p + 1, h]], axis=1).astype(BF16)
                qh = q[:, h * HEAD_DIM_ATT:(h + 1) * HEAD_DIM_ATT]
                parts.append(jnp.dot(qh, kt, preferred_element_type=F32) + bias)
            sc = jnp.concatenate(parts, axis=0)
            s_scr[c] = sc[:, :page]
            s_scr[c + 1] = sc[:, page:]
            macc = jnp.maximum(macc, jnp.maximum(sc[:, :page], sc[:, page:]))
        return macc

    macc = lax.fori_loop(0, n_groups, k_body, macc)
    m = jnp.max(macc, axis=1, keepdims=True)

    def pv_chunk(vt_of_head, c):
        pch = jnp.exp(s_scr[c] - m)
        pb = pch.astype(BF16)
        outs = [lax.dot_general(pb[h * tq:(h + 1) * tq], vt_of_head(h).astype(BF16), NT_DIMS,
                                preferred_element_type=F32) for h in range(N_HEADS_ATT)]
        return pch, jnp.concatenate(outs, axis=0)

    vnew_t = new_t(vnew_ref[...])
    lacc, oacc = pv_chunk(lambda h: vnew_t[h * HEAD_DIM_ATT:(h + 1) * HEAD_DIM_ATT], n_pages)

    def v_body(g, carry):
        lacc, oacc = carry
        slot = lax.rem(n_groups + g, 2)

        @pl.when(g + 1 < n_groups)
        def _():
            start_group(cv_hbm, g + 1, 1 - slot)

        wait_group(cv_hbm, g, slot)
        for p in range(0, pg, 2):
            c = g * pg + p
            pch = jnp.exp(jnp.concatenate([s_scr[c], s_scr[c + 1]], axis=1) - m)
            pb = pch.astype(BF16)
            outs = []
            for h in range(N_HEADS_ATT):
                vt = jnp.concatenate([kv_buf[slot, p, h], kv_buf[slot, p + 1, h]], axis=1).astype(BF16)
                outs.append(lax.dot_general(pb[h * tq:(h + 1) * tq], vt, NT_DIMS, preferred_element_type=F32))
            lacc = lacc + (pch[:, :page] + pch[:, page:])
            oacc = oacc + jnp.concatenate(outs, axis=0)
        return lacc, oacc

    lacc, oacc = lax.fori_loop(0, n_groups, v_body, (lacc, oacc))
    o = oacc / jnp.sum(lacc, axis=1, keepdims=True)
    o_ref[...] = jnp.concatenate([o[h * tq:(h + 1) * tq] for h in range(N_HEADS_ATT)], axis=1)


def _dsa_sample(layer, page_table, qi, wcol, ki_new, q, k_new, v_new, cache_ki_t, cache_k_t, cache_v_t, n_sel, pg):
    b, n_pages = page_table.shape
    page = cache_k_t.shape[-1]
    assert page == LANES and n_pages % pg == 0 and ki_new.shape[1] == SUBLANES
    per_b = lambda shape: pl.BlockSpec((None,) + shape, lambda i, pt: (i,) + (0,) * len(shape))
    hbm = pl.BlockSpec(memory_space=pl.ANY)
    hq_idx = N_IDX_HEADS * SUBLANES
    hq = N_HEADS_ATT * SUBLANES
    grid_spec = pltpu.PrefetchScalarGridSpec(
        num_scalar_prefetch=1,
        grid=(b,),
        in_specs=[per_b((hq_idx, IDX_DIM)), per_b((hq_idx, 1)), per_b((SUBLANES, IDX_DIM)), per_b((SUBLANES, ATT_W)),
                  per_b((SUBLANES, ATT_W)), per_b((SUBLANES, ATT_W)), hbm, hbm, hbm],
        out_specs=per_b((SUBLANES, ATT_W)),
        scratch_shapes=[pltpu.VMEM((n_pages, IDX_DIM, page), F32),
                        pltpu.VMEM((2, pg, N_HEADS_ATT, HEAD_DIM_ATT, page), F32),
                        pltpu.SemaphoreType.DMA((1,)), pltpu.SemaphoreType.DMA((2,)),
                        pltpu.VMEM((n_pages + 1, SUBLANES, LANES), F32), pltpu.VMEM((n_pages + 1, hq, LANES), F32),
                        pltpu.VMEM((SUBLANES, LANES), I32)],
    )
    return pl.pallas_call(
        functools.partial(_dsa_sample_kernel, n_sel, n_pages, pg, layer),
        grid_spec=grid_spec,
        out_shape=jax.ShapeDtypeStruct((b, SUBLANES, ATT_W), F32),
        compiler_params=_params(1),
        name="dsa_sample",
    )(page_table, qi, wcol, ki_new, q, k_new, v_new, cache_ki_t, cache_k_t, cache_v_t)


def _regroup_w_in(w):
    o_qi = 3 * ATT_W
    o_ki = o_qi + IDX_W
    o_wi = o_ki + IDX_DIM
    o_cv = o_wi + N_IDX_HEADS
    o_a = o_cv + CONV_CH
    o_b = o_a + N_HEADS_GDN
    o_z = o_b + N_HEADS_GDN
    gate_cols = jnp.concatenate([w[:, o_wi:o_cv], w[:, o_a:o_b], w[:, o_b:o_z]], axis=1)
    gate_cols = jnp.pad(gate_cols, ((0, 0), (0, LANES - gate_cols.shape[1])))
    out = jnp.concatenate([w[:, :o_qi], w[:, o_cv:o_a], w[:, o_z:o_z + GDN_V], w[:, o_qi:o_ki],
                           w[:, o_ki:o_wi], w[:, o_ki:o_wi], gate_cols], axis=1)
    assert out.shape[1] == PJ_W
    return out.astype(BF16)


def _gate_row(v):
    return jnp.pad(v.astype(F32), (GATE_A, LANES - GATE_A - v.shape[0]))[None, :]


def _tiles(bp, tp, bs, ts, n_pages):
    ck = min(512, tp)
    return dict(
        tm_p=ck,
        tm_s=min(256, bs * ts),
        tq=min(512, tp),
        ck=ck,
        bb_p=min(4, bp),
        bb_s=min(4, bs),
        pg=min(16, n_pages),
    )


def kernel(x_prompt, x_sample, cache_k, cache_v, cache_kidx, state_ssm, state_conv, page_table, norm_mix, w_in,
           conv_w, a_log, dt_bias, norm_gdn_out, w_out, norm_mlp, w_up, w_down, norm_final):
    depth = w_in.shape[0]
    bp, tp, d = x_prompt.shape
    bs, ts, _ = x_sample.shape
    page = cache_k.shape[2]
    past = page_table.shape[1] * page
    cfg = _tiles(bp, tp, bs, ts, page_table.shape[1])
    n_sel_p = min(TOP_K_MAX, tp // 4)
    n_sel_s = min(TOP_K_MAX, (past + ts) // 4)
    assert tp % CHUNK == 0 and CONV_W - 1 <= ts <= CHUNK
    assert cfg["tq"] >= n_sel_p and tp % cfg["tq"] == 0 and tp % cfg["ck"] == 0 and cfg["ck"] % cfg["tq"] == 0
    assert bp % cfg["bb_p"] == 0 and bs % cfg["bb_s"] == 0

    cache_k_t = jnp.transpose(cache_k, (0, 1, 3, 4, 2))
    cache_v_t = jnp.transpose(cache_v, (0, 1, 3, 4, 2))
    cache_ki_t = jnp.transpose(cache_kidx, (0, 1, 3, 2))

    yp = x_prompt.reshape(bp * tp, d)
    ys = x_sample.reshape(bs * ts, d)
    outs = {k: [] for k in ("kp", "vp", "kip", "sp", "cp", "ks", "vs", "kis", "ss", "cs")}
    for l in range(depth):
        w_pj = _regroup_w_in(w_in[l])
        g_mix = norm_mix[l][None, :]
        alog_row, dtb_row = _gate_row(a_log[l]), _gate_row(dt_bias[l])
        gn = norm_gdn_out[l][None, :]
        w_out_b, w_up_b, w_dn_b = w_out[l].astype(BF16), w_up[l].astype(BF16), w_down[l].astype(BF16)
        g_mlp = norm_mlp[l][None, :]
        g_fin = norm_final[None, :]
        last = l == depth - 1

        qab, kab, vtb, kat, vat, kit, qkv, z, qib, ki2b, gates = _inproj(yp, g_mix, w_pj, cfg["tm_p"], tp, True)
        oa = _dsa_prompt(qib, ki2b, gates, qab, kab, vtb, bp, tp, n_sel_p, cfg["tq"], cfg["ck"])
        ob, s_new = _gdn(qkv.reshape(bp, tp, CONV_CH), jnp.zeros((bp, CONV_W - 1, CONV_CH), F32),
                         gates.reshape(bp, tp, LANES), z.reshape(bp, tp, GDN_V),
                         jnp.zeros((bp, N_HEADS_GDN, HEAD_DIM_K, HEAD_DIM_V), F32),
                         conv_w[l], alog_row, dtb_row, gn, CHUNK, cfg["bb_p"])
        outs["kp"].append(jnp.transpose(kat, (0, 3, 1, 2)))
        outs["vp"].append(jnp.transpose(vat, (0, 3, 1, 2)))
        outs["kip"].append(jnp.transpose(kit, (0, 2, 1)))
        outs["sp"].append(s_new.astype(state_ssm.dtype))
        outs["cp"].append(qkv.reshape(bp, tp, CONV_CH)[:, tp - (CONV_W - 1):, :])
        yp = _mlp_layer(yp, oa, ob.reshape(bp * tp, GDN_V), w_out_b, g_mlp, w_up_b, w_dn_b, g_fin, last, cfg["tm_p"])

        qab, ka, va, ki, qkv, z, qib, gates = _inproj(ys, g_mix, w_pj, cfg["tm_s"], ts, False)
        qi_rows = qib.reshape(bs, ts, N_IDX_HEADS, IDX_DIM).transpose(0, 2, 1, 3).reshape(bs, N_IDX_HEADS * ts, IDX_DIM)
        wcol = gates[:, GATE_WI:GATE_WI + N_IDX_HEADS].reshape(bs, ts, N_IDX_HEADS)
        wcol = wcol.transpose(0, 2, 1).reshape(bs, N_IDX_HEADS * ts, 1)
        oa = _dsa_sample(l, page_table, qi_rows, wcol, ki.reshape(bs, ts, IDX_DIM), qab.reshape(bs, ts, ATT_W),
                         ka.reshape(bs, ts, ATT_W), va.reshape(bs, ts, ATT_W),
                         cache_ki_t, cache_k_t, cache_v_t, n_sel_s, cfg["pg"])
        pad_t = lambda a: jnp.pad(a.reshape(bs, ts, a.shape[-1]), ((0, 0), (0, CHUNK - ts), (0, 0)))
        ob, s_new = _gdn(pad_t(qkv), state_conv[l], pad_t(gates), pad_t(z), state_ssm[l].astype(F32),
                         conv_w[l], alog_row, dtb_row, gn, ts, cfg["bb_s"])
        outs["ks"].append(ka.reshape(bs, ts, N_HEADS_ATT, HEAD_DIM_ATT))
        outs["vs"].append(va.reshape(bs, ts, N_HEADS_ATT, HEAD_DIM_ATT))
        outs["kis"].append(ki.reshape(bs, ts, IDX_DIM))
        outs["ss"].append(s_new.astype(state_ssm.dtype))
        outs["cs"].append(qkv.reshape(bs, ts, CONV_CH)[:, ts - (CONV_W - 1):, :])
        ys = _mlp_layer(ys, oa.reshape(bs * ts, ATT_W).astype(BF16), ob[:, :ts, :].reshape(bs * ts, GDN_V),
                        w_out_b, g_mlp, w_up_b, w_dn_b, g_fin, last, cfg["tm_s"])

    st = lambda k: jnp.stack(outs[k])
    return (yp.reshape(bp, tp, d), ys.reshape(bs, ts, d),
            st("kp"), st("vp"), st("kip"), st("sp"), st("cp"),
            st("ks"), st("vs"), st("kis"), st("ss"), st("cs"))
```

```python
import functools

import numpy as np
import jax
import jax.numpy as jnp
from jax import lax
from jax.experimental import pallas as pl
from jax.experimental.pallas import tpu as pltpu

F32 = jnp.float32
BF16 = jnp.bfloat16
I32 = jnp.int32
HI = lax.Precision.HIGHEST

EPS = 1e-6
N_HEADS_ATT = 8
HEAD_DIM_ATT = 64
ATT_W = N_HEADS_ATT * HEAD_DIM_ATT
N_IDX_HEADS = 4
IDX_DIM = 64
IDX_W = N_IDX_HEADS * IDX_DIM
TOP_K_MAX = 256
N_HEADS_GDN = 4
HEAD_DIM_K = 128
HEAD_DIM_V = 128
GDN_K = N_HEADS_GDN * HEAD_DIM_K
GDN_V = N_HEADS_GDN * HEAD_DIM_V
CONV_W = 4
CONV_CH = 2 * GDN_K + GDN_V
CHUNK = 64

LANES = 128
SUBLANES = 8
VMEM_LIMIT = 56 * 1024 * 1024
N_ACC = 4

PJ_ATT = 0
PJ_CV = PJ_ATT + 3 * ATT_W
PJ_Z = PJ_CV + CONV_CH
PJ_QI = PJ_Z + GDN_V
PJ_KI = PJ_QI + IDX_W
PJ_GATE = PJ_KI + 2 * IDX_DIM
PJ_W = PJ_GATE + LANES
GATE_WI = 0
GATE_A = 4
GATE_B = 8

INT_MIN = np.int32(-2 ** 31)
KEY_NEG = np.int32(np.uint32(0x807FFFFF).view(np.int32))
NEG_INF = float("-inf")

NT_DIMS = (((1,), (1,)), ((), ()))
TN_DIMS = (((0,), (0,)), ((), ()))


def _key_to_float(key):
    bits = jnp.where(key < 0, key ^ np.int32(0x7FFFFFFF), key)
    return jnp.where(key < KEY_NEG, NEG_INF, lax.bitcast_convert_type(bits, F32))


def _sigmoid(x):
    return 1.0 / (1.0 + jnp.exp(-x))


def _softplus(x):
    return jnp.maximum(x, 0.0) + jnp.log1p(jnp.exp(-jnp.abs(x)))


def _split_bf16(x):
    hi = x.astype(BF16)
    return hi, (x - hi.astype(F32)).astype(BF16)


def _dot_split(a, b):
    dot = lambda p, q: jnp.dot(p, q, preferred_element_type=F32)
    return dot(a[0], b[0]) + (dot(a[0], b[1]) + dot(a[1], b[0]))


def _rms(x, g):
    return x * lax.rsqrt(jnp.mean(x * x, axis=-1, keepdims=True) + EPS) * g


def _const_spec(shape):
    return pl.BlockSpec(shape, lambda *_: (0,) * len(shape), pipeline_mode=pl.Buffered(1))


def _params(n_grid_dims):
    return pltpu.CompilerParams(dimension_semantics=("arbitrary",) * n_grid_dims, vmem_limit_bytes=VMEM_LIMIT)


def _inproj_kernel(cache_t, x_ref, g_ref, w_ref, *out_refs):
    hb = _rms(x_ref[...], g_ref[...]).astype(BF16)
    proj = jnp.dot(hb, w_ref[...], preferred_element_type=F32)
    qa = proj[:, PJ_ATT:PJ_ATT + ATT_W]
    ka = proj[:, PJ_ATT + ATT_W:PJ_ATT + 2 * ATT_W]
    va = proj[:, PJ_ATT + 2 * ATT_W:PJ_ATT + 3 * ATT_W]
    qkv = proj[:, PJ_CV:PJ_CV + CONV_CH]
    z = proj[:, PJ_Z:PJ_Z + GDN_V]
    qi = proj[:, PJ_QI:PJ_QI + IDX_W]
    ki2 = proj[:, PJ_KI:PJ_KI + 2 * IDX_DIM]
    gates = proj[:, PJ_GATE:PJ_GATE + LANES]
    if cache_t:
        (qab_ref, kab_ref, vtb_ref, kat_ref, vat_ref, kit_ref, qkv_ref, z_ref, qib_ref, ki2b_ref, gate_ref) = out_refs
        tm = ka.shape[0]
        kab_ref[...] = ka.astype(BF16)
        ki2b_ref[...] = ki2.astype(BF16)
        vat = va.T
        vtb_ref[...] = vat.astype(BF16)
        kat_ref[...] = ka.T.reshape(N_HEADS_ATT, HEAD_DIM_ATT, tm)
        vat_ref[...] = vat.reshape(N_HEADS_ATT, HEAD_DIM_ATT, tm)
        kit_ref[...] = ki2.T[:IDX_DIM, :]
    else:
        (qab_ref, ka_ref, va_ref, ki_ref, qkv_ref, z_ref, qib_ref, gate_ref) = out_refs
        ka_ref[...] = ka
        va_ref[...] = va
        ki_ref[...] = ki2[:, :IDX_DIM]
    qab_ref[...] = qa.astype(BF16)
    qkv_ref[...] = qkv
    z_ref[...] = z
    qib_ref[...] = qi.astype(BF16)
    gate_ref[...] = gates


def _inproj(x2d, g, w, tm, seq_len, cache_t):
    n, d = x2d.shape
    row = lambda w_: pl.BlockSpec((tm, w_), lambda i: (i, 0))
    sds = jax.ShapeDtypeStruct
    common = [(row(CONV_CH), sds((n, CONV_CH), F32)), (row(GDN_V), sds((n, GDN_V), F32)),
              (row(IDX_W), sds((n, IDX_W), BF16)), (row(LANES), sds((n, LANES), F32))]
    if cache_t:
        assert seq_len % tm == 0
        n_t = seq_len // tm
        b = n // seq_len
        tok = lambda *lead: pl.BlockSpec((None,) + lead + (tm,), lambda i: (i // n_t,) + (0,) * len(lead) + (i % n_t,))
        head_t = (N_HEADS_ATT, HEAD_DIM_ATT)
        out = [(row(ATT_W), sds((n, ATT_W), BF16)),
               (row(ATT_W), sds((n, ATT_W), BF16)),
               (pl.BlockSpec((None, None, ATT_W, tm), lambda i: (i // n_t, i % n_t, 0, 0)),
                sds((b, n_t, ATT_W, tm), BF16)),
               (tok(*head_t), sds((b,) + head_t + (seq_len,), F32)),
               (tok(*head_t), sds((b,) + head_t + (seq_len,), F32)),
               (tok(IDX_DIM), sds((b, IDX_DIM, seq_len), F32)),
               common[0], common[1], common[2],
               (row(2 * IDX_DIM), sds((n, 2 * IDX_DIM), BF16)),
               common[3]]
    else:
        out = [(row(ATT_W), sds((n, ATT_W), BF16)), (row(ATT_W), sds((n, ATT_W), F32)),
               (row(ATT_W), sds((n, ATT_W), F32)), (row(IDX_DIM), sds((n, IDX_DIM), F32))] + common
    return pl.pallas_call(
        functools.partial(_inproj_kernel, cache_t),
        grid=(n // tm,),
        in_specs=[row(d), _const_spec((1, d)), _const_spec(w.shape)],
        out_specs=tuple(o[0] for o in out),
        out_shape=tuple(o[1] for o in out),
        compiler_params=_params(1),
        name="inproj",
    )(x2d, g, w)


def _mlp_kernel(ff_slab, final_norm, x_ref, oa_ref, ob_ref, wout_ref, gm_ref, wup_ref, wdn_ref, gf_ref, y_ref):
    mix = jnp.concatenate([oa_ref[...], ob_ref[...]], axis=1)
    x1 = x_ref[...] + jnp.dot(mix, wout_ref[...], preferred_element_type=F32)
    hm = _rms(x1, gm_ref[...]).astype(BF16)
    acc = x1
    d_ff = wup_ref.shape[1]
    for s in range(0, d_ff, ff_slab):
        up = jnp.dot(hm, wup_ref[:, s:s + ff_slab], preferred_element_type=F32)
        act = jnp.square(jnp.maximum(up, 0.0)).astype(BF16)
        acc = acc + jnp.dot(act, wdn_ref[s:s + ff_slab, :], preferred_element_type=F32)
    y_ref[...] = _rms(acc, gf_ref[...]) if final_norm else acc


def _mlp_layer(x2d, oa, ob, w_out, g_mlp, w_up, w_dn, g_final, final_norm, tm):
    n, d = x2d.shape
    row = lambda w: pl.BlockSpec((tm, w), lambda i: (i, 0))
    return pl.pallas_call(
        functools.partial(_mlp_kernel, 1024, final_norm),
        grid=(n // tm,),
        in_specs=[row(d), row(ATT_W), row(GDN_V), _const_spec(w_out.shape), _const_spec((1, d)),
                  _const_spec(w_up.shape), _const_spec(w_dn.shape), _const_spec((1, d))],
        out_specs=row(d),
        out_shape=jax.ShapeDtypeStruct((n, d), F32),
        compiler_params=_params(1),
        name="outproj_mlp",
    )(x2d, oa, ob, w_out, g_mlp, w_up, w_dn, g_final)


def _gdn_kernel(t_valid, bb, qkv_ref, conv0_ref, gate_ref, z_ref, s0_ref, cw_ref, alog_ref, dtb_ref, gn_ref,
                ob_ref, sout_ref, ext_ref, s_ref):
    c = pl.program_id(1)
    n_c = pl.num_programs(1)
    C = CHUNK
    HIST = SUBLANES

    @pl.when(c == 0)
    def _():
        for b in range(bb):
            ext_ref[b, 0:HIST, :] = jnp.zeros((HIST, CONV_CH), F32)
            ext_ref[b, HIST - (CONV_W - 1):HIST, :] = conv0_ref[b]
        s_ref[...] = s0_ref[...]

    H = N_HEADS_GDN
    R = H * C
    rows = lax.broadcasted_iota(I32, (R, R), 0)
    cols = lax.broadcasted_iota(I32, (R, R), 1)
    same_head = (rows // C) == (cols // C)
    incl = jnp.logical_and(same_head, rows >= cols)
    strict = jnp.logical_and(same_head, rows > cols)
    l_incl = (lax.broadcasted_iota(I32, (C, C), 0) >= lax.broadcasted_iota(I32, (C, C), 1)).astype(F32)
    gn = gn_ref[...]
    n_double = int(np.log2(C)) - 1
    stack = lambda f: jnp.concatenate([f(h) for h in range(H)], axis=0)
    head = lambda x, h: x[h * C:(h + 1) * C]

    convs, s_olds = [], []
    for b in range(bb):
        ext_ref[b, HIST:HIST + C, :] = qkv_ref[b]
        conv = ext_ref[b, HIST - 3:HIST - 3 + C, :] * cw_ref[0:1, :]
        for j in range(1, CONV_W):
            conv = conv + ext_ref[b, HIST - 3 + j:HIST - 3 + j + C, :] * cw_ref[j:j + 1, :]
        convs.append(conv * _sigmoid(conv))
        ext_ref[b, 0:HIST, :] = ext_ref[b, C:C + HIST, :]
        s_olds.append([s_ref[b, h] for h in range(H)])

    prep = []
    for b in range(bb):
        conv = convs[b]
        gates = gate_ref[b]
        g_slab = -jnp.exp(alog_ref[...]) * _softplus(gates + dtb_ref[...])
        beta_slab = _sigmoid(gates)
        if t_valid < C:
            live = lax.broadcasted_iota(I32, (C, LANES), 0) < t_valid
            g_slab = jnp.where(live, g_slab, 0.0)
            beta_slab = jnp.where(live, beta_slab, 0.0)
        gc_slab = jnp.dot(l_incl, g_slab, precision=HI, preferred_element_type=F32)
        gc_rows = gc_slab.T
        z = z_ref[b]

        q = stack(lambda h: conv[:, h * HEAD_DIM_K:(h + 1) * HEAD_DIM_K])
        k = stack(lambda h: conv[:, GDN_K + h * HEAD_DIM_K:GDN_K + (h + 1) * HEAD_DIM_K])
        v = stack(lambda h: conv[:, 2 * GDN_K + h * HEAD_DIM_V:2 * GDN_K + (h + 1) * HEAD_DIM_V])
        zz = stack(lambda h: z[:, h * HEAD_DIM_V:(h + 1) * HEAD_DIM_V])
        q = q * lax.rsqrt(jnp.sum(q * q, axis=-1, keepdims=True) + EPS) * (HEAD_DIM_K ** -0.5)
        k = k * lax.rsqrt(jnp.sum(k * k, axis=-1, keepdims=True) + EPS)
        gc = stack(lambda h: gc_slab[:, GATE_A + h:GATE_A + h + 1])
        beta = stack(lambda h: beta_slab[:, GATE_B + h:GATE_B + h + 1])
        gc_row = jnp.concatenate([gc_rows[GATE_A + h:GATE_A + h + 1, :] for h in range(H)], axis=1)
        g_last = stack(lambda h: jnp.broadcast_to(gc_slab[C - 1:C, GATE_A + h:GATE_A + h + 1], (C, 1)))
        eg = jnp.exp(gc)
        kb = k * beta
        vb = v * beta
        prep.append(dict(q=q, k=k, kb=kb, gdiff=gc - gc_row, r=jnp.concatenate([vb, kb * eg], axis=1), qe=q * eg,
                         kd=k * jnp.exp(g_last - gc), g_last=g_last, zz=zz))

    each = range(bb)
    kk = [lax.dot_general(prep[b]["kb"], prep[b]["k"], NT_DIMS, preferred_element_type=F32) for b in each]
    qk = [lax.dot_general(prep[b]["q"], prep[b]["k"], NT_DIMS, preferred_element_type=F32) for b in each]
    for b in each:
        decay = jnp.exp(jnp.where(incl, prep[b]["gdiff"], NEG_INF))
        prep[b]["x"] = -jnp.where(strict, kk[b] * decay, 0.0)
        prep[b]["aqk"] = jnp.where(incl, qk[b] * decay, 0.0)

    xs = [_split_bf16(prep[b]["x"]) for b in each]
    r = [prep[b]["r"] + _dot_split(xs[b], _split_bf16(prep[b]["r"])) for b in each]
    pw = [_dot_split(xs[b], xs[b]) for b in each]
    r = [r[b] + _dot_split(_split_bf16(pw[b]), _split_bf16(r[b])) for b in each]
    for _ in range(n_double - 1):
        pw = [jnp.dot(pw[b], pw[b], preferred_element_type=F32) for b in each]
        r = [r[b] + jnp.dot(pw[b], r[b], preferred_element_type=F32) for b in each]

    dot = lambda a, c: jnp.dot(a, c, preferred_element_type=F32)
    ws = [stack(lambda h, b=b: dot(head(r[b][:, HEAD_DIM_V:], h), s_olds[b][h])) for b in each]
    qs = [stack(lambda h, b=b: dot(head(prep[b]["qe"], h), s_olds[b][h])) for b in each]
    v_new = [r[b][:, :HEAD_DIM_V] - ws[b] for b in each]
    o = [qs[b] + dot(prep[b]["aqk"], v_new[b]) for b in each]
    s_news = [[s_olds[b][h] * jnp.exp(prep[b]["g_last"][h * C:h * C + 1, :])
               + lax.dot_general(head(prep[b]["kd"], h), head(v_new[b], h), TN_DIMS, preferred_element_type=F32)
               for h in range(H)] for b in each]
    o_outs = [(_rms(o[b], gn) * (prep[b]["zz"] * _sigmoid(prep[b]["zz"]))).astype(ob_ref.dtype) for b in each]

    for b in range(bb):
        for h in range(H):
            s_ref[b, h] = s_news[b][h]
            ob_ref[b, :, h * HEAD_DIM_V:(h + 1) * HEAD_DIM_V] = head(o_outs[b], h)

    @pl.when(c == n_c - 1)
    def _():
        sout_ref[...] = s_ref[...]


def _gdn(qkv, conv0, gates, z, s0, conv_w, alog_row, dtb_row, gn, t_valid, bb):
    b, t, _ = qkv.shape
    n_c = t // CHUNK
    blk = lambda w: pl.BlockSpec((bb, CHUNK, w), lambda i, c: (i, c, 0))
    per_b = lambda shape: pl.BlockSpec((bb,) + shape, lambda i, c: (i,) + (0,) * len(shape))
    state = (N_HEADS_GDN, HEAD_DIM_K, HEAD_DIM_V)
    return pl.pallas_call(
        functools.partial(_gdn_kernel, t_valid, bb),
        grid=(b // bb, n_c),
        in_specs=[blk(CONV_CH), per_b((CONV_W - 1, CONV_CH)), blk(LANES), blk(GDN_V), per_b(state),
                  _const_spec((CONV_W, CONV_CH)), _const_spec((1, LANES)), _const_spec((1, LANES)),
                  _const_spec((1, HEAD_DIM_V))],
        out_specs=(blk(GDN_V), per_b(state)),
        out_shape=(jax.ShapeDtypeStruct((b, t, GDN_V), BF16), jax.ShapeDtypeStruct((b,) + state, F32)),
        scratch_shapes=[pltpu.VMEM((bb, CHUNK + 2 * SUBLANES, CONV_CH), F32), pltpu.VMEM((bb,) + state, F32)],
        compiler_params=_params(2),
        name="gdn",
    )(qkv, conv0, gates, z, s0, conv_w, alog_row, dtb_row, gn)


def _topk_threshold(count_ge, qshape, n_sel):
    def bis(it, u):
        cu = u | lax.shift_left(jnp.int32(1), 31 - it)
        return jnp.where(count_ge(_key_to_float(cu ^ INT_MIN)) >= n_sel, cu, u)

    thr_key = lax.fori_loop(0, 32, bis, jnp.zeros(qshape, I32)) ^ INT_MIN
    return thr_key, _key_to_float(thr_key), _key_to_float(thr_key + 1)


def _topk_rule(count_ge, count_tie_le, qshape, n_sel, n_idx_bits, j_scr):
    thr_key, thr, nxt = _topk_threshold(count_ge, qshape, n_sel)
    need = n_sel - count_ge(nxt)
    excess = jnp.logical_and(count_ge(thr) > n_sel, thr_key > KEY_NEG)
    j_scr[...] = jnp.full(qshape, np.int32(2 ** 30), I32)

    @pl.when(jnp.max(excess.astype(I32)) > 0)
    def _():
        def jbis(it, j):
            cj = j | lax.shift_left(jnp.int32(1), n_idx_bits - 1 - it)
            return jnp.where(count_tie_le(thr, nxt, cj - 1) < need, cj, j)

        j_scr[...] = lax.fori_loop(0, n_idx_bits, jbis, jnp.zeros(qshape, I32))

    return thr, nxt, j_scr[...]


def _dsa_prompt_kernel(n_sel, tq, ck, t_total, qi_ref, ki2_ref, gate_ref, q_ref, k_ref, vt_ref,
                       o_ref, sc_scr, s_scr_even, s_scr_odd):
    s_scrs = (s_scr_even, s_scr_odd)
    i = pl.program_id(1)
    n_chunks = lax.div(i * tq + tq + ck - 1, ck)
    grp = ck // SUBLANES
    g3 = (grp, SUBLANES, tq)
    q_id = i * tq + lax.broadcasted_iota(I32, g3, 2)
    k_in_chunk = lax.broadcasted_iota(I32, g3, 0) * SUBLANES + lax.broadcasted_iota(I32, g3, 1)
    lane_lo = lax.broadcasted_iota(I32, (tq, LANES), 1) < HEAD_DIM_ATT
    rep8 = lambda v: jnp.broadcast_to(v, (SUBLANES, tq))

    qi = qi_ref[...] * (IDX_DIM ** -0.5)
    wi_t = gate_ref[...].T[GATE_WI:GATE_WI + N_IDX_HEADS, :] * (N_IDX_HEADS ** -0.5)
    qi_heads = []
    for h in range(N_IDX_HEADS):
        pair = qi[:, (h // 2) * LANES:(h // 2 + 1) * LANES]
        keep = lane_lo if h % 2 == 0 else jnp.logical_not(lane_lo)
        qi_heads.append(jnp.where(keep, pair, jnp.zeros_like(pair)))

    def score_body(c, carry):
        k0 = pl.multiple_of(c * ck, ck)
        kic = ki2_ref[pl.ds(k0, ck), :]
        sc = jnp.zeros((ck, tq), F32)
        for h in range(N_IDX_HEADS):
            logit = lax.dot_general(kic, qi_heads[h], NT_DIMS, preferred_element_type=F32)
            sc = sc + jnp.maximum(logit, 0.0) * wi_t[h:h + 1, :]
        causal = k0 + k_in_chunk <= q_id
        sc_scr[c] = jnp.where(causal, sc.reshape(g3), NEG_INF).reshape(ck, tq)
        return carry

    lax.fori_loop(0, n_chunks, score_body, 0)

    n_acc = max(1, N_ACC * LANES // tq)
    acc_shape = (n_acc, SUBLANES, tq)
    fold = lambda x, op: op(x.reshape(grp // n_acc, n_acc, SUBLANES, tq), axis=0)
    finish = lambda acc, op: op(op(acc, axis=0), axis=0, keepdims=True)

    def count_ge(cand):
        cand8 = rep8(cand)
        body = lambda c, acc: acc + fold(jnp.where(sc_scr[c].reshape(g3) >= cand8, 1.0, 0.0), jnp.sum)
        return finish(lax.fori_loop(0, n_chunks, body, jnp.zeros(acc_shape, F32)), jnp.sum)

    _, thr, nxt = _topk_threshold(count_ge, (1, tq), n_sel)
    need = n_sel - count_ge(nxt)
    tri = (lax.broadcasted_iota(I32, (ck, ck), 0) >= lax.broadcasted_iota(I32, (ck, ck), 1)).astype(BF16)

    def mask_body(c, seen):
        s = sc_scr[c]
        above = s >= nxt
        tie = jnp.logical_and(s >= thr, jnp.logical_not(above))
        rank = seen + jnp.dot(tri, jnp.where(tie, 1.0, 0.0).astype(BF16), preferred_element_type=F32)
        sel = jnp.logical_or(above, jnp.logical_and(tie, rank <= need))
        sel = jnp.logical_and(sel, s > NEG_INF)
        sc_scr[c] = jnp.where(sel, 0.0, NEG_INF)
        return rank[ck - 1:ck, :]

    lax.fori_loop(0, n_chunks, mask_body, jnp.zeros((1, tq), F32))

    def q_masked(h):
        qp = q_ref[:, (h // 2) * LANES:(h // 2 + 1) * LANES] * (HEAD_DIM_ATT ** -0.5)
        keep = lane_lo if h % 2 == 0 else jnp.logical_not(lane_lo)
        return jnp.where(keep, qp, jnp.zeros_like(qp))

    sub = ck
    fold_sub = lambda x, op: op(x.reshape(sub // (n_acc * SUBLANES), n_acc, SUBLANES, tq), axis=0)

    def s_pass(h, qm, c, macc):
        k0 = pl.multiple_of(c * ck, ck)
        for r in range(0, ck, sub):
            kc = k_ref[pl.ds(k0 + r, sub), (h // 2) * LANES:(h // 2 + 1) * LANES]
            st = lax.dot_general(kc, qm, NT_DIMS, preferred_element_type=F32) + sc_scr[c, r:r + sub, :]
            s_scrs[h % 2][c, r:r + sub, :] = st
            macc = jnp.maximum(macc, fold_sub(st, jnp.max))
        return macc

    def pv_pass(h, m, c, lacc, oacc):
        for r in range(0, ck, sub):
            pt = jnp.exp(s_scrs[h % 2][c, r:r + sub, :] - m)
            vt = vt_ref[c, h * HEAD_DIM_ATT:(h + 1) * HEAD_DIM_ATT, r:r + sub]
            lacc = lacc + fold_sub(pt, jnp.sum)
            oacc = oacc + jnp.dot(vt, pt.astype(BF16), preferred_element_type=F32)
        return lacc, oacc

    neg = jnp.full(acc_shape, NEG_INF, F32)
    zero_l = jnp.zeros(acc_shape, F32)
    zero_o = jnp.zeros((HEAD_DIM_ATT, tq), F32)
    qm0 = q_masked(0)
    macc = lax.fori_loop(0, n_chunks, lambda c, a: s_pass(0, qm0, c, a), neg)
    o_rows = []
    for h in range(N_HEADS_ATT):
        m = finish(macc, jnp.max)
        if h + 1 < N_HEADS_ATT:
            qm_next = q_masked(h + 1)

            def body(c, carry, h=h, m=m, qm_next=qm_next):
                lacc, oacc, macc_next = carry
                macc_next = s_pass(h + 1, qm_next, c, macc_next)
                lacc, oacc = pv_pass(h, m, c, lacc, oacc)
                return lacc, oacc, macc_next

            lacc, oacc, macc = lax.fori_loop(0, n_chunks, body, (zero_l, zero_o, neg))
        else:
            lacc, oacc = lax.fori_loop(0, n_chunks, lambda c, cr, h=h, m=m: pv_pass(h, m, c, *cr), (zero_l, zero_o))
        o_rows.append(oacc / finish(lacc, jnp.sum))
    o_ref[...] = jnp.concatenate(o_rows, axis=0).T.astype(o_ref.dtype)


def _dsa_prompt(qib, ki2b, gates, qab, kab, vtb, batch, t, n_sel, tq, ck):
    n_q = t // tq
    tile = lambda w: pl.BlockSpec((tq, w), lambda b, i: (b * n_q + i, 0))
    seq = lambda w: pl.BlockSpec((t, w), lambda b, i: (b, 0))
    return pl.pallas_call(
        functools.partial(_dsa_prompt_kernel, n_sel, tq, ck, t),
        grid=(batch, n_q),
        in_specs=[tile(IDX_W), seq(2 * IDX_DIM), tile(LANES), tile(ATT_W), seq(ATT_W),
                  pl.BlockSpec((None, t // ck, ATT_W, ck), lambda b, i: (b, 0, 0, 0))],
        out_specs=tile(ATT_W),
        out_shape=jax.ShapeDtypeStruct((batch * t, ATT_W), BF16),
        scratch_shapes=[pltpu.VMEM((t // ck, ck, tq), F32), pltpu.VMEM((t // ck, ck, tq), F32),
                        pltpu.VMEM((t // ck, ck, tq), F32)],
        compiler_params=_params(2),
        name="dsa_prompt",
    )(qib, ki2b, gates, qab, kab, vtb)


def _dsa_sample_kernel(n_sel, n_pages, pg, layer, pt_ref, qi_ref, wcol_ref, kinew_ref, q_ref, knew_ref, vnew_ref,
                       cki_hbm, ck_hbm, cv_hbm, o_ref, ki_buf, kv_buf, sem_ki, sem_kv, sc_scr, s_scr, j_scr):
    b = pl.program_id(0)
    n_groups = n_pages // pg
    n_ch = n_pages + 1
    tq = SUBLANES
    page = LANES
    qshape = (tq, page)
    lane = lax.broadcasted_iota(I32, qshape, 1)

    def ki_copy(p):
        return pltpu.make_async_copy(cki_hbm.at[layer, pt_ref[b, p]], ki_buf.at[p], sem_ki.at[0])

    def kv_copy(src, g, p, slot):
        return pltpu.make_async_copy(src.at[layer, pt_ref[b, g * pg + p]], kv_buf.at[slot, p], sem_kv.at[slot])

    def start_group(src, g, slot):
        for p in range(pg):
            kv_copy(src, g, p, slot).start()

    def wait_group(src, g, slot):
        for p in range(pg):
            kv_copy(src, g, p, slot).wait()

    for p in range(n_pages):
        ki_copy(p).start()
    start_group(ck_hbm, 0, 0)

    def new_t(x):
        wdt = x.shape[1]
        wp = max(wdt, page)
        if wp > wdt:
            x = jnp.concatenate([x, jnp.zeros((tq, wp - wdt), F32)], axis=1)
        sq = jnp.concatenate([x, jnp.zeros((page - tq, wp), F32)], axis=0)
        return jnp.concatenate([sq[:, t * page:(t + 1) * page].T for t in range(wp // page)], axis=0)[:wdt]

    qi = qi_ref[...] * (IDX_DIM ** -0.5)
    wcol = wcol_ref[...] * (N_IDX_HEADS ** -0.5)

    def score_chunk(ki_t):
        logit = jnp.dot(qi, ki_t.astype(BF16), preferred_element_type=F32)
        lw = jnp.maximum(logit, 0.0) * wcol
        sc = lw[0:tq]
        for h in range(1, N_IDX_HEADS):
            sc = sc + lw[h * tq:(h + 1) * tq]
        return sc

    sc_new = score_chunk(new_t(kinew_ref[...]))
    sc_scr[n_pages] = jnp.where(lane <= lax.broadcasted_iota(I32, qshape, 0), sc_new, NEG_INF)
    for p in range(n_pages):
        ki_copy(p).wait()
    for p in range(n_pages):
        sc_scr[p] = score_chunk(ki_buf[p])

    def row_total(hits):
        accs = [jnp.zeros(qshape, F32)] * N_ACC
        for c in range(n_ch):
            accs[c % N_ACC] = accs[c % N_ACC] + jnp.where(hits(c), 1.0, 0.0)
        tot = functools.reduce(lambda x, y: x + y, accs)
        return jnp.broadcast_to(jnp.sum(tot, axis=1, keepdims=True), qshape)

    def count_ge(cand):
        return row_total(lambda c: sc_scr[c] >= cand)

    def tie(sc, thr, nxt, c, j):
        return jnp.logical_and(jnp.logical_and(sc >= thr, jnp.logical_not(sc >= nxt)), c * page + lane <= j)

    def count_tie_le(thr, nxt, j):
        return row_total(lambda c: tie(sc_scr[c], thr, nxt, c, j))

    thr, nxt, jmax = _topk_rule(count_ge, count_tie_le, qshape, n_sel, int(np.ceil(np.log2(n_ch * page))), j_scr)

    def bias_body(c, carry):
        sc = sc_scr[c]
        sel = jnp.logical_or(sc >= nxt, tie(sc, thr, nxt, c, jmax))
        sel = jnp.logical_and(sel, sc > NEG_INF)
        sc_scr[c] = jnp.where(sel, 0.0, NEG_INF)
        return carry

    lax.fori_loop(0, n_ch, bias_body, 0)

    q = q_ref[...] * (HEAD_DIM_ATT ** -0.5)

    def s_chunk(kt_of_head, c):
        bias = sc_scr[c]
        parts = []
        for h in range(N_HEADS_ATT):
            qh = q[:, h * HEAD_DIM_ATT:(h + 1) * HEAD_DIM_ATT]
            parts.append(jnp.dot(qh, kt_of_head(h).astype(BF16), preferred_element_type=F32) + bias)
        sc = jnp.concatenate(parts, axis=0)
        s_scr[c] = sc
        return sc

    knew_t = new_t(knew_ref[...])
    macc = s_chunk(lambda h: knew_t[h * HEAD_DIM_ATT:(h + 1) * HEAD_DIM_ATT], n_pages)

    def k_body(g, macc):
        slot = lax.rem(g, 2)

        @pl.when(g + 1 < n_groups)
        def _():
            start_group(ck_hbm, g + 1, 1 - slot)

        @pl.when(g + 1 == n_groups)
        def _():
            start_group(cv_hbm, 0, 1 - slot)

        wait_group(ck_hbm, g, slot)
        for p in range(0, pg, 2):
            c = g * pg + p
            bias = jnp.concatenate([sc_scr[c], sc_scr[c + 1]], axis=1)
            parts = []
            for h in range(N_HEADS_ATT):
                kt = jnp.concatenate([kv_buf[slot, p, h], kv_buf[slot, p + 1, h]], axis=1).astype(BF16)
                qh = q[:, h * HEAD_DIM_ATT:(h + 1) * HEAD_DIM_ATT]
                parts.append(jnp.dot(qh, kt, preferred_element_type=F32) + bias)
            sc = jnp.concatenate(parts, axis=0)
            s_scr[c] = sc[:, :page]
            s_scr[c + 1] = sc[:, page:]
            macc = jnp.maximum(macc, jnp.maximum(sc[:, :page], sc[:, page:]))
        return macc

    macc = lax.fori_loop(0, n_groups, k_body, macc)
    m = jnp.max(macc, axis=1, keepdims=True)

    def pv_chunk(vt_of_head, c):
        pch = jnp.exp(s_scr[c] - m)
        pb = pch.astype(BF16)
        outs = [lax.dot_general(pb[h * tq:(h + 1) * tq], vt_of_head(h).astype(BF16), NT_DIMS,
                                preferred_element_type=F32) for h in range(N_HEADS_ATT)]
        return pch, jnp.concatenate(outs, axis=0)

    vnew_t = new_t(vnew_ref[...])
    lacc, oacc = pv_chunk(lambda h: vnew_t[h * HEAD_DIM_ATT:(h + 1) * HEAD_DIM_ATT], n_pages)

    def v_body(g, carry):
        lacc, oacc = carry
        slot = lax.rem(n_groups + g, 2)

        @pl.when(g + 1 < n_groups)
        def _():
            start_group(cv_hbm, g + 1, 1 - slot)

        wait_group(cv_hbm, g, slot)
        for p in range(0, pg, 2):
            c = g * pg + p
            pch = jnp.exp(jnp.concatenate([s_scr[c], s_scr[c + 1]], axis=1) - m)
            pb = pch.astype(BF16)
            outs = []
            for h in range(N_HEADS_ATT):
                vt = jnp.concatenate([kv_buf[slot, p, h], kv_buf[slot, p + 1, h]], axis=1).astype(BF16)
                outs.append(lax.dot_general(pb[h * tq:(h + 1) * tq], vt, NT_DIMS, preferred_element_type=F32))
            lacc = lacc + (pch[:, :page] + pch[:, page:])
            oacc = oacc + jnp.concatenate(outs, axis=0)
        return lacc, oacc

    lacc, oacc = lax.fori_loop(0, n_groups, v_body, (lacc, oacc))
    o = oacc / jnp.sum(lacc, axis=1, keepdims=True)
    o_ref[...] = jnp.concatenate([o[h * tq:(h + 1) * tq] for h in range(N_HEADS_ATT)], axis=1)


def _dsa_sample(layer, page_table, qi, wcol, ki_new, q, k_new, v_new, cache_ki_t, cache_k_t, cache_v_t, n_sel, pg):
    b, n_pages = page_table.shape
    page = cache_k_t.shape[-1]
    assert page == LANES and n_pages % pg == 0 and ki_new.shape[1] == SUBLANES
    per_b = lambda shape: pl.BlockSpec((None,) + shape, lambda i, pt: (i,) + (0,) * len(shape))
    hbm = pl.BlockSpec(memory_space=pl.ANY)
    hq_idx = N_IDX_HEADS * SUBLANES
    hq = N_HEADS_ATT * SUBLANES
    grid_spec = pltpu.PrefetchScalarGridSpec(
        num_scalar_prefetch=1,
        grid=(b,),
        in_specs=[per_b((hq_idx, IDX_DIM)), per_b((hq_idx, 1)), per_b((SUBLANES, IDX_DIM)), per_b((SUBLANES, ATT_W)),
                  per_b((SUBLANES, ATT_W)), per_b((SUBLANES, ATT_W)), hbm, hbm, hbm],
        out_specs=per_b((SUBLANES, ATT_W)),
        scratch_shapes=[pltpu.VMEM((n_pages, IDX_DIM, page), F32),
                        pltpu.VMEM((2, pg, N_HEADS_ATT, HEAD_DIM_ATT, page), F32),
                        pltpu.SemaphoreType.DMA((1,)), pltpu.SemaphoreType.DMA((2,)),
                        pltpu.VMEM((n_pages + 1, SUBLANES, LANES), F32), pltpu.VMEM((n_pages + 1, hq, LANES), F32),
                        pltpu.VMEM((SUBLANES, LANES), I32)],
    )
    return pl.pallas_call(
        functools.partial(_dsa_sample_kernel, n_sel, n_pages, pg, layer),
        grid_spec=grid_spec,
        out_shape=jax.ShapeDtypeStruct((b, SUBLANES, ATT_W), F32),
        compiler_params=_params(1),
        name="dsa_sample",
    )(page_table, qi, wcol, ki_new, q, k_new, v_new, cache_ki_t, cache_k_t, cache_v_t)


def _regroup_w_in(w):
    o_qi = 3 * ATT_W
    o_ki = o_qi + IDX_W
    o_wi = o_ki + IDX_DIM
    o_cv = o_wi + N_IDX_HEADS
    o_a = o_cv + CONV_CH
    o_b = o_a + N_HEADS_GDN
    o_z = o_b + N_HEADS_GDN
    gate_cols = jnp.concatenate([w[:, o_wi:o_cv], w[:, o_a:o_b], w[:, o_b:o_z]], axis=1)
    gate_cols = jnp.pad(gate_cols, ((0, 0), (0, LANES - gate_cols.shape[1])))
    out = jnp.concatenate([w[:, :o_qi], w[:, o_cv:o_a], w[:, o_z:o_z + GDN_V], w[:, o_qi:o_ki],
                           w[:, o_ki:o_wi], w[:, o_ki:o_wi], gate_cols], axis=1)
    assert out.shape[1] == PJ_W
    return out.astype(BF16)


def _gate_row(v):
    return jnp.pad(v.astype(F32), (GATE_A, LANES - GATE_A - v.shape[0]))[None, :]


def _tiles(bp, tp, bs, ts, n_pages):
    ck = min(512, tp)
    return dict(
        tm_p=ck,
        tm_s=min(256, bs * ts),
        tq=min(512, tp),
        ck=ck,
        bb_p=min(4, bp),
        bb_s=min(4, bs),
        pg=min(16, n_pages),
    )


def kernel(x_prompt, x_sample, cache_k, cache_v, cache_kidx, state_ssm, state_conv, page_table, norm_mix, w_in,
           conv_w, a_log, dt_bias, norm_gdn_out, w_out, norm_mlp, w_up, w_down, norm_final):
    depth = w_in.shape[0]
    bp, tp, d = x_prompt.shape
    bs, ts, _ = x_sample.shape
    page = cache_k.shape[2]
    past = page_table.shape[1] * page
    cfg = _tiles(bp, tp, bs, ts, page_table.shape[1])
    n_sel_p = min(TOP_K_MAX, tp // 4)
    n_sel_s = min(TOP_K_MAX, (past + ts) // 4)
    assert tp % CHUNK == 0 and CONV_W - 1 <= ts <= CHUNK
    assert cfg["tq"] >= n_sel_p and tp % cfg["tq"] == 0 and tp % cfg["ck"] == 0 and cfg["ck"] % cfg["tq"] == 0
    assert bp % cfg["bb_p"] == 0 and bs % cfg["bb_s"] == 0

    cache_k_t = jnp.transpose(cache_k, (0, 1, 3, 4, 2))
    cache_v_t = jnp.transpose(cache_v, (0, 1, 3, 4, 2))
    cache_ki_t = jnp.transpose(cache_kidx, (0, 1, 3, 2))

    yp = x_prompt.reshape(bp * tp, d)
    ys = x_sample.reshape(bs * ts, d)
    outs = {k: [] for k in ("kp", "vp", "kip", "sp", "cp", "ks", "vs", "kis", "ss", "cs")}
    for l in range(depth):
        w_pj = _regroup_w_in(w_in[l])
        g_mix = norm_mix[l][None, :]
        alog_row, dtb_row = _gate_row(a_log[l]), _gate_row(dt_bias[l])
        gn = norm_gdn_out[l][None, :]
        w_out_b, w_up_b, w_dn_b = w_out[l].astype(BF16), w_up[l].astype(BF16), w_down[l].astype(BF16)
        g_mlp = norm_mlp[l][None, :]
        g_fin = norm_final[None, :]
        last = l == depth - 1

        qab, kab, vtb, kat, vat, kit, qkv, z, qib, ki2b, gates = _inproj(yp, g_mix, w_pj, cfg["tm_p"], tp, True)
        oa = _dsa_prompt(qib, ki2b, gates, qab, kab, vtb, bp, tp, n_sel_p, cfg["tq"], cfg["ck"])
        ob, s_new = _gdn(qkv.reshape(bp, tp, CONV_CH), jnp.zeros((bp, CONV_W - 1, CONV_CH), F32),
                         gates.reshape(bp, tp, LANES), z.reshape(bp, tp, GDN_V),
                         jnp.zeros((bp, N_HEADS_GDN, HEAD_DIM_K, HEAD_DIM_V), F32),
                         conv_w[l], alog_row, dtb_row, gn, CHUNK, cfg["bb_p"])
        outs["kp"].append(jnp.transpose(kat, (0, 3, 1, 2)))
        outs["vp"].append(jnp.transpose(vat, (0, 3, 1, 2)))
        outs["kip"].append(jnp.transpose(kit, (0, 2, 1)))
        outs["sp"].append(s_new.astype(state_ssm.dtype))
        outs["cp"].append(qkv.reshape(bp, tp, CONV_CH)[:, tp - (CONV_W - 1):, :])
        yp = _mlp_layer(yp, oa, ob.reshape(bp * tp, GDN_V), w_out_b, g_mlp, w_up_b, w_dn_b, g_fin, last, cfg["tm_p"])

        qab, ka, va, ki, qkv, z, qib, gates = _inproj(ys, g_mix, w_pj, cfg["tm_s"], ts, False)
        qi_rows = qib.reshape(bs, ts, N_IDX_HEADS, IDX_DIM).transpose(0, 2, 1, 3).reshape(bs, N_IDX_HEADS * ts, IDX_DIM)
        wcol = gates[:, GATE_WI:GATE_WI + N_IDX_HEADS].reshape(bs, ts, N_IDX_HEADS)
        wcol = wcol.transpose(0, 2, 1).reshape(bs, N_IDX_HEADS * ts, 1)
        oa = _dsa_sample(l, page_table, qi_rows, wcol, ki.reshape(bs, ts, IDX_DIM), qab.reshape(bs, ts, ATT_W),
                         ka.reshape(bs, ts, ATT_W), va.reshape(bs, ts, ATT_W),
                         cache_ki_t, cache_k_t, cache_v_t, n_sel_s, cfg["pg"])
        pad_t = lambda a: jnp.pad(a.reshape(bs, ts, a.shape[-1]), ((0, 0), (0, CHUNK - ts), (0, 0)))
        ob, s_new = _gdn(pad_t(qkv), state_conv[l], pad_t(gates), pad_t(z), state_ssm[l].astype(F32),
                         conv_w[l], alog_row, dtb_row, gn, ts, cfg["bb_s"])
        outs["ks"].append(ka.reshape(bs, ts, N_HEADS_ATT, HEAD_DIM_ATT))
        outs["vs"].append(va.reshape(bs, ts, N_HEADS_ATT, HEAD_DIM_ATT))
        outs["kis"].append(ki.reshape(bs, ts, IDX_DIM))
        outs["ss"].append(s_new.astype(state_ssm.dtype))
        outs["cs"].append(qkv.reshape(bs, ts, CONV_CH)[:, ts - (CONV_W - 1):, :])
        ys = _mlp_layer(ys, oa.reshape(bs * ts, ATT_W).astype(BF16), ob[:, :ts, :].reshape(bs * ts, GDN_V),
                        w_out_b, g_mlp, w_up_b, w_dn_b, g_fin, last, cfg["tm_s"])

    st = lambda k: jnp.stack(outs[k])
    return (yp.reshape(bp, tp, d), ys.reshape(bs, ts, d),
            st("kp"), st("vp"), st("kip"), st("sp"), st("cp"),
            st("ks"), st("vs"), st("kis"), st("ss"), st("cs"))
```
